```python
import jax
import jax.numpy as jnp
from jax import lax
import numpy as np

D_MODEL = 1024
BATCH = 4
SEQ = 8192
DEPTH = 4

GRID_W = 64
CTX_LEN = 256
D_CONV = 512
N_CONV_GROUPS = 8
D_REC = 512
N_REC_HEADS = 8
REC_HEAD_DIM = D_REC // N_REC_HEADS
CONV_W = 3
REC_CONV_W = 4
LRU_C = 8.0
D_IN = 3 * D_CONV + 2 * D_REC
D_MIX = D_CONV + D_REC
F_DENSE = 2816
F_MOE = 3584
N_EXPERTS = 8
TOP_K = 2
EPS = 1e-6
N_DENSE = (DEPTH + 1) // 2
N_MOE = DEPTH // 2

kernel_name = "hybrid_conv_rglru_moe_dit"


def rmsnorm(x, g):
    x32 = x.astype(jnp.float32)
    y = x32 * lax.rsqrt(jnp.mean(x32 * x32, axis=-1, keepdims=True) + EPS)
    return (y * g.astype(jnp.float32)).astype(x.dtype)


def modulate(h, shift, scale):
    return h * (1 + scale) + shift


def shift_conv(u, w, axis, pad_left):
    k_w = w.shape[0]
    n = u.shape[axis]
    pad = [(0, 0)] * u.ndim
    pad[axis] = (pad_left, k_w - 1 - pad_left)
    up = jnp.pad(u, pad)
    out = w[0] * lax.slice_in_dim(up, 0, n, axis=axis)
    for k in range(1, k_w):
        out = out + w[k] * lax.slice_in_dim(up, k, k + n, axis=axis)
    return out


def split_proj(p):
    return jnp.split(p, [D_CONV, 2 * D_CONV, 3 * D_CONV, 3 * D_CONV + D_REC], axis=-1)


def short_conv_latent(u, w):
    b, n, ch = u.shape
    rows = n // GRID_W
    g = u.reshape(b, rows, GRID_W, ch)
    half = ch // 2
    horiz = shift_conv(g[..., :half], w[:, :half], axis=2, pad_left=1)
    vert = shift_conv(g[..., half:], w[:, half:], axis=1, pad_left=1)
    return jnp.concatenate([horiz, vert], axis=-1).reshape(b, n, ch)


def block_diag(x, w):
    b, t, _ = x.shape
    xh = x.reshape(b, t, N_REC_HEADS, REC_HEAD_DIM)
    return jnp.einsum("bthi,hij->bthj", xh, w).reshape(b, t, D_REC)


def lru_coeffs(xr, w_a, b_a, w_i, b_i, lam):
    f32 = jnp.float32
    x32 = xr.astype(f32)
    r = jax.nn.sigmoid(block_diag(x32, w_a.astype(f32)) + b_a.astype(f32))
    i = jax.nn.sigmoid(block_diag(x32, w_i.astype(f32)) + b_i.astype(f32))
    log_a = -LRU_C * r * jax.nn.softplus(-lam.astype(f32))
    a = jnp.exp(log_a)
    mult = jnp.sqrt(-jnp.expm1(2.0 * log_a))
    return a, mult * i * x32


def linear_scan(a, b, h0, reverse):
    if reverse:
        b = b.at[:, -1].add(a[:, -1] * h0)
    else:
        b = b.at[:, 0].add(a[:, 0] * h0)

    def combine(lft, rgt):
        return (lft[0] * rgt[0], rgt[0] * lft[1] + rgt[1])

    _, h = lax.associative_scan(combine, (a, b), reverse=reverse, axis=1)
    return h


def bidir_rglru(xr_c, xr_l, wa, ba, wi, bi, lam, need_ctx):
    zero = jnp.zeros((xr_c.shape[0], D_REC), jnp.float32)
    h_c_sum = None
    h_l_sum = None
    for d, rev in ((0, False), (1, True)):
        a, b = lru_coeffs(xr_c, wa[d], ba[d], wi[d], bi[d], lam[d])
        h_c = linear_scan(a, b, zero, rev)
        state = h_c[:, 0] if rev else h_c[:, -1]
        a, b = lru_coeffs(xr_l, wa[d], ba[d], wi[d], bi[d], lam[d])
        h_l = linear_scan(a, b, state, rev)
        h_l_sum = h_l if h_l_sum is None else h_l_sum + h_l
        if need_ctx:
            h_c_sum = h_c if h_c_sum is None else h_c_sum + h_c
    return h_c_sum, h_l_sum


def swiglu(h, wg, wu, wd):
    return (jax.nn.silu(h @ wg) * (h @ wu)) @ wd


def moe(h, router_w, router_b, wg, wu, wd):
    logits = (h @ router_w).astype(jnp.float32) + router_b.astype(jnp.float32)
    top_v, top_i = lax.top_k(logits, TOP_K)
    probs = jax.nn.softmax(top_v, axis=-1)
    gates = jnp.sum(jax.nn.one_hot(top_i, N_EXPERTS, dtype=jnp.float32) * probs[..., None], axis=-2)
    gates = gates.astype(h.dtype)
    y = jnp.zeros_like(h)
    for e in range(N_EXPERTS):
        y = y + gates[..., e:e + 1] * swiglu(h, wg[e], wu[e], wd[e])
    return y


def setup_inputs(seed: int = 0) -> dict:
    key = jax.random.key(seed)
    ks = iter(jax.random.split(key, 40))
    f32 = jnp.float32
    D = D_MODEL

    def nrm(shape, scale):
        return jax.random.normal(next(ks), shape, f32) * scale

    u = jax.random.uniform(next(ks), (DEPTH, 2, D_REC), f32, 0.9, 0.999)
    a0 = u ** (1.0 / LRU_C)
    lam = jnp.log(a0) - jnp.log1p(-a0)
    return {
        "x": nrm((BATCH, SEQ, D), 1.0),
        "c": nrm((BATCH, D), 1.0),
        "ctx": nrm((BATCH, CTX_LEN, D), 1.0),
        "c_ctx": nrm((D,), 1.0),
        "ada_w": nrm((DEPTH, D, 6 * D), 0.5 * D ** -0.5),
        "ada_b": nrm((DEPTH, 6 * D), 0.02),
        "norm_mix_g": 1.0 + nrm((DEPTH, D), 0.02),
        "norm_ffn_g": 1.0 + nrm((DEPTH, D), 0.02),
        "w_in": nrm((DEPTH, D, D_IN), D ** -0.5),
        "w_out": nrm((DEPTH, D_MIX, D), D_MIX ** -0.5),
        "conv_w": nrm((DEPTH, CONV_W, D_CONV), CONV_W ** -0.5),
        "rec_conv_w": nrm((DEPTH, REC_CONV_W, D_REC), REC_CONV_W ** -0.5),
        "rec_conv_b": nrm((DEPTH, D_REC), 0.02),
        "lru_wa": nrm((DEPTH, 2, N_REC_HEADS, REC_HEAD_DIM, REC_HEAD_DIM), REC_HEAD_DIM ** -0.5),
        "lru_ba": nrm((DEPTH, 2, D_REC), 0.02),
        "lru_wi": nrm((DEPTH, 2, N_REC_HEADS, REC_HEAD_DIM, REC_HEAD_DIM), REC_HEAD_DIM ** -0.5),
        "lru_bi": nrm((DEPTH, 2, D_REC), 0.02),
        "lru_lam": lam,
        "dense_wg": nrm((N_DENSE, D, F_DENSE), D ** -0.5),
        "dense_wu": nrm((N_DENSE, D, F_DENSE), D ** -0.5),
        "dense_wd": nrm((N_DENSE, F_DENSE, D), F_DENSE ** -0.5),
        "router_w": nrm((N_MOE, D, N_EXPERTS), D ** -0.5),
        "router_b": nrm((N_MOE, N_EXPERTS), 0.01),
        "moe_wg": nrm((N_MOE, N_EXPERTS, D, F_MOE), D ** -0.5),
        "moe_wu": nrm((N_MOE, N_EXPERTS, D, F_MOE), D ** -0.5),
        "moe_wd": nrm((N_MOE, N_EXPERTS, F_MOE, D), F_MOE ** -0.5),
        "final_g": 1.0 + nrm((D,), 0.02),
    }


def reference(x, c, ctx, c_ctx, ada_w, ada_b, norm_mix_g, norm_ffn_g, w_in, w_out, conv_w,
              rec_conv_w, rec_conv_b, lru_wa, lru_ba, lru_wi, lru_bi, lru_lam,
              dense_wg, dense_wu, dense_wd, router_w, router_b, moe_wg, moe_wu, moe_wd, final_g):
    rec_x_lo = 3 * D_CONV
    xl = x
    xc = ctx
    silu_c = jax.nn.silu(c)
    silu_cc = jax.nn.silu(c_ctx)

    for l in range(DEPTH):
        last = l == DEPTH - 1
        mod_l = silu_c @ ada_w[l] + ada_b[l]
        mod_c = silu_cc @ ada_w[l] + ada_b[l]
        shm_l, scm_l, gm_l, shf_l, scf_l, gf_l = [m[:, None] for m in jnp.split(mod_l, 6, axis=-1)]
        shm_c, scm_c, gm_c, shf_c, scf_c, gf_c = jnp.split(mod_c, 6, axis=-1)

        hl = modulate(rmsnorm(xl, norm_mix_g[l]), shm_l, scm_l)
        hc = modulate(rmsnorm(xc, norm_mix_g[l]), shm_c, scm_c)
        cb_l, cc_l, cv_l, rx_l, rg_l = split_proj(hl @ w_in[l])
        if last:
            rx_c = hc @ w_in[l][:, rec_x_lo:rec_x_lo + D_REC]
        else:
            cb_c, cc_c, cv_c, rx_c, rg_c = split_proj(hc @ w_in[l])

        xr_l = shift_conv(rx_l, rec_conv_w[l], axis=1, pad_left=2) + rec_conv_b[l]
        xr_c = shift_conv(rx_c, rec_conv_w[l], axis=1, pad_left=2) + rec_conv_b[l]
        h_c, h_l = bidir_rglru(xr_c, xr_l, lru_wa[l], lru_ba[l], lru_wi[l], lru_bi[l], lru_lam[l],
                               need_ctx=not last)

        conv_out_l = cb_l * short_conv_latent(cc_l * cv_l, conv_w[l])
        rec_out_l = h_l.astype(xl.dtype) * jax.nn.gelu(rg_l)
        out_l = jnp.concatenate([conv_out_l, rec_out_l], axis=-1) @ w_out[l]
        xl = xl + gm_l * out_l
        if not last:
            conv_out_c = cb_c * shift_conv(cc_c * cv_c, conv_w[l], axis=1, pad_left=1)
            rec_out_c = h_c.astype(xc.dtype) * jax.nn.gelu(rg_c)
            out_c = jnp.concatenate([conv_out_c, rec_out_c], axis=-1) @ w_out[l]
            xc = xc + gm_c * out_c

        def channel_mix(h):
            if l % 2 == 0:
                j = l // 2
                return swiglu(h, dense_wg[j], dense_wu[j], dense_wd[j])
            j = l // 2
            return moe(h, router_w[j], router_b[j], moe_wg[j], moe_wu[j], moe_wd[j])

        hl2 = modulate(rmsnorm(xl, norm_ffn_g[l]), shf_l, scf_l)
        xl = xl + gf_l * channel_mix(hl2)
        if not last:
            hc2 = modulate(rmsnorm(xc, norm_ffn_g[l]), shf_c, scf_c)
            xc = xc + gf_c * channel_mix(hc2)

    return rmsnorm(xl, final_g)
```

```python
import functools
import math

import jax
import jax.numpy as jnp
from jax import lax
from jax.experimental import pallas as pl
from jax.experimental.pallas import tpu as pltpu

GRID_W = 64
LRU_C = 8.0
EPS = 1e-6
REC_HEAD_DIM = 64

LANES = 128
SEG = 128
SEG_PITCH = SEG + 8
CTX_NSEG = 8
MAX_GROUP = 32
ROUTER_LANES = 128

F32 = jnp.float32
BF16 = jnp.bfloat16


def _cparams(n_grid, vmem_mb):
    return pltpu.CompilerParams(
        dimension_semantics=("arbitrary",) * n_grid,
        vmem_limit_bytes=vmem_mb * 1024 * 1024)


def _dot(a, b):
    return jnp.dot(a, b, preferred_element_type=F32)


def _sigmoid(z):
    return 1.0 / (1.0 + jnp.exp(-z))


def _split_bf16(v):
    hi = v.astype(BF16)
    lo = (v - hi.astype(F32)).astype(BF16)
    return hi, lo


def _dot3(a, b):
    a_hi, a_lo = _split_bf16(a)
    b_hi, b_lo = _split_bf16(b)
    return _dot(a_hi, b_hi) + _dot(a_hi, b_lo) + _dot(a_lo, b_hi)


def _norm_mod(x, g, shift, scale):
    ms = jnp.mean(x * x, axis=-1, keepdims=True)
    y = x * lax.rsqrt(ms + EPS) * g
    return y * (1.0 + scale) + shift


def _ada_kernel(c_ref, w_ref, b_ref, o_ref):
    c = c_ref[...]
    s = c * _sigmoid(c)
    o_ref[...] = _dot3(s, w_ref[...]) + b_ref[...]


def _ada_call(cvec, ada_w, ada_b):
    depth, d, d6 = ada_w.shape
    tn = d6 // 4
    return pl.pallas_call(
        _ada_kernel,
        grid=(depth, d6 // tn),
        in_specs=[
            pl.BlockSpec((8, d), lambda l, j: (0, 0)),
            pl.BlockSpec((None, d, tn), lambda l, j: (l, 0, j)),
            pl.BlockSpec((None, 1, tn), lambda l, j: (l, 0, j)),
        ],
        out_specs=pl.BlockSpec((None, 8, tn), lambda l, j: (l, 0, j)),
        out_shape=jax.ShapeDtypeStruct((depth, 8, d6), F32),
        compiler_params=_cparams(2, 48),
        name="ada_mod",
    )(cvec, ada_w, ada_b.reshape(depth, 1, d6))


def _proj_kernel(dh, x_ref, mod_ref, g_ref, w_ref, cb_ref, u_ref, rx_ref, gl_ref):
    h = _norm_mod(x_ref[...], g_ref[...], mod_ref[0:1, :], mod_ref[1:2, :]).astype(BF16)

    def proj(j):
        return _dot(h, w_ref[:, j * dh:(j + 1) * dh])

    cb_ref[...] = proj(0).astype(BF16)
    u_ref[...] = (proj(1) * proj(2)).astype(BF16)
    rx_ref[...] = proj(3).astype(BF16)
    rg = proj(4)
    gelu = 0.5 * rg * (1.0 + jnp.tanh(math.sqrt(2.0 / math.pi) * (rg + 0.044715 * rg * rg * rg)))
    gl_ref[...] = gelu.astype(BF16)


def _proj_call(xres, mod_l, norm_g, w_in, *, tm, n_tiles, first_block, mod_index):
    d = xres.shape[1]
    d_in = w_in.shape[1]
    dh = d_in // 5
    out = jax.ShapeDtypeStruct((n_tiles * tm, dh), BF16)
    ospec = pl.BlockSpec((tm, dh), lambda i: (i, 0))
    return pl.pallas_call(
        functools.partial(_proj_kernel, dh),
        grid=(n_tiles,),
        in_specs=[
            pl.BlockSpec((tm, d), lambda i: (first_block + i, 0)),
            pl.BlockSpec((None, 6, d), lambda i: (mod_index(i), 0, 0)),
            pl.BlockSpec((1, d), lambda i: (0, 0)),
            pl.BlockSpec((d, d_in), lambda i: (0, 0)),
        ],
        out_specs=[ospec, ospec, ospec, ospec],
        out_shape=[out, out, out, out],
        compiler_params=_cparams(1, 56),
        name="mix_proj",
    )(xres, mod_l, norm_g, w_in)


def _scan_kernel(n_lat, n_ctx, rxl_ref, rxc_ref, gl_ref, gc_ref, cw_ref, cbias_ref, wg_ref,
                 gb_ref, lam_ref, recl_ref, recc_ref, xpl_ref, xpc_ref, ab_ref, cin_ref):
    nseg_l = n_lat // SEG
    seg_c = n_ctx // CTX_NSEG
    pitch_c = seg_c + 8
    base_l = CTX_NSEG * pitch_c
    group = min(MAX_GROUP, nseg_l)
    n_groups = nseg_l // group

    def stage(src_ref, dst_ref, n):
        dst_ref[0:8, :] = jnp.zeros((8, LANES), F32)
        dst_ref[n + 8:n + 16, :] = jnp.zeros((8, LANES), F32)
        rows = min(512, n)

        def body(k, _):
            r0 = pl.multiple_of(k * rows, rows)
            dst_ref[pl.ds(r0 + 8, rows), :] = src_ref[pl.ds(r0, rows), :].astype(F32)
            return 0

        lax.fori_loop(0, n // rows, body, 0)

    stage(rxl_ref, xpl_ref, n_lat)
    stage(rxc_ref, xpc_ref, n_ctx)

    lam = lam_ref[...]
    softplus = jnp.maximum(-lam, 0.0) + jnp.log1p(jnp.exp(-jnp.abs(lam)))
    cvec = -LRU_C * softplus
    cw = cw_ref[...]
    cbias = cbias_ref[...]
    gb = gb_ref[...]

    def gates(xp_ref, t0, nrows, dst):
        win = xp_ref[pl.ds(t0, nrows + 16), :]
        xr = (cw[0:1] * win[6:6 + nrows] + cw[1:2] * win[7:7 + nrows]
              + cw[2:3] * win[8:8 + nrows] + cw[3:4] * win[9:9 + nrows] + cbias)
        z = _dot(xr.astype(BF16), wg_ref[...]) + gb
        for d in range(2):
            r = _sigmoid(z[:, 2 * d * LANES:(2 * d + 1) * LANES])
            gi = _sigmoid(z[:, (2 * d + 1) * LANES:(2 * d + 2) * LANES])
            a = jnp.exp(cvec[d:d + 1] * r)
            b = jnp.sqrt((1.0 - a) * (1.0 + a)) * (gi * xr)
            ab_ref[2 * d, pl.ds(dst, nrows), :] = a
            ab_ref[2 * d + 1, pl.ds(dst, nrows), :] = b

    def gates_ctx(j, _):
        gates(xpc_ref, pl.multiple_of(j * seg_c, 8), seg_c, pl.multiple_of(j * pitch_c, 8))
        return 0

    def gates_lat(j, _):
        gates(xpl_ref, pl.multiple_of(j * SEG, SEG), SEG,
              pl.multiple_of(base_l + j * SEG_PITCH, 8))
        return 0

    lax.fori_loop(0, CTX_NSEG, gates_ctx, 0)
    lax.fori_loop(0, nseg_l, gates_lat, 0)

    def scan_group(d, base, nseg, seg_len, pitch, seg0, carry, reverse):
        a_ref = ab_ref.at[2 * d]
        b_ref = ab_ref.at[2 * d + 1]

        def step(k, hc):
            h, acc = hc
            s = (seg_len - 1 - k) if reverse else k
            idx = pl.ds(base + s, nseg, stride=pitch)
            a = a_ref[idx, :]
            b = b_ref[idx, :]
            h = a * h + b
            acc = a * acc
            b_ref[idx, :] = h
            a_ref[idx, :] = acc
            return h, acc

        h_end, a_end = lax.fori_loop(
            0, seg_len, step, (jnp.zeros((nseg, LANES), F32), jnp.ones((nseg, LANES), F32)))
        order = range(nseg - 1, -1, -1) if reverse else range(nseg)
        for j in order:
            cin_ref[d, seg0 + j:seg0 + j + 1, :] = carry
            carry = h_end[j:j + 1] + a_end[j:j + 1] * carry
        return carry

    for d, reverse in ((0, False), (1, True)):
        carry = jnp.zeros((1, LANES), F32)
        carry = scan_group(d, 0, CTX_NSEG, seg_c, pitch_c, 0, carry, reverse)
        groups = range(n_groups - 1, -1, -1) if reverse else range(n_groups)
        for gidx in groups:
            carry = scan_group(d, base_l + gidx * group * SEG_PITCH, group, SEG, SEG_PITCH,
                               CTX_NSEG + gidx * group, carry, reverse)

    def emit(g_ref, out_ref, t0, nrows, src, seg):
        c0 = cin_ref[0, pl.ds(seg, 1), :]
        c1 = cin_ref[1, pl.ds(seg, 1), :]
        h = (ab_ref[1, pl.ds(src, nrows), :] + ab_ref[0, pl.ds(src, nrows), :] * c0
             + ab_ref[3, pl.ds(src, nrows), :] + ab_ref[2, pl.ds(src, nrows), :] * c1)
        g = g_ref[pl.ds(t0, nrows), :].astype(F32)
        out_ref[pl.ds(t0, nrows), :] = (h * g).astype(BF16)

    def emit_ctx(j, _):
        emit(gc_ref, recc_ref, pl.multiple_of(j * seg_c, 8), seg_c,
             pl.multiple_of(j * pitch_c, 8), j)
        return 0

    def emit_lat(j, _):
        emit(gl_ref, recl_ref, pl.multiple_of(j * SEG, SEG), SEG,
             pl.multiple_of(base_l + j * SEG_PITCH, 8), CTX_NSEG + j)
        return 0

    lax.fori_loop(0, CTX_NSEG, emit_ctx, 0)
    lax.fori_loop(0, nseg_l, emit_lat, 0)


def _scan_call(rx_l, rx_c, g_l, g_c, conv_w, conv_b, wgate, bgate, lam, *, batch):
    n_lat = rx_l.shape[0] // batch
    n_ctx = rx_c.shape[0] // batch
    d_rec = rx_l.shape[1]
    ncb = d_rec // LANES
    seg_c = n_ctx // CTX_NSEG
    ab_rows = CTX_NSEG * (seg_c + 8) + (n_lat // SEG) * SEG_PITCH
    nseg_tot = CTX_NSEG + n_lat // SEG
    lat = pl.BlockSpec((n_lat, LANES), lambda b, c: (b, c))
    ctx = pl.BlockSpec((n_ctx, LANES), lambda b, c: (b, c))
    return pl.pallas_call(
        functools.partial(_scan_kernel, n_lat, n_ctx),
        grid=(batch, ncb),
        in_specs=[
            lat, ctx, lat, ctx,
            pl.BlockSpec((4, LANES), lambda b, c: (0, c)),
            pl.BlockSpec((1, LANES), lambda b, c: (0, c)),
            pl.BlockSpec((None, LANES, 4 * LANES), lambda b, c: (c, 0, 0)),
            pl.BlockSpec((None, 1, 4 * LANES), lambda b, c: (c, 0, 0)),
            pl.BlockSpec((None, 2, LANES), lambda b, c: (c, 0, 0)),
        ],
        out_specs=[lat, ctx],
        out_shape=[jax.ShapeDtypeStruct(rx_l.shape, BF16),
                   jax.ShapeDtypeStruct(rx_c.shape, BF16)],
        scratch_shapes=[
            pltpu.VMEM((n_lat + 16, LANES), F32),
            pltpu.VMEM((n_ctx + 16, LANES), F32),
            pltpu.VMEM((4, ab_rows, LANES), F32),
            pltpu.VMEM((2, nseg_tot, LANES), F32),
        ],
        compiler_params=_cparams(2, 56),
        name="rglru_scan",
    )(rx_l, rx_c, g_l, g_c, conv_w, conv_b, wgate, bgate, lam)


def _gate_params(wa, ba, wi, bi, lam):
    n_heads = wa.shape[1]
    hpb = LANES // REC_HEAD_DIM
    ncb = n_heads // hpb

    def blockdiag(w):
        w = w.reshape(ncb, hpb, REC_HEAD_DIM, REC_HEAD_DIM)
        rows = []
        for p in range(hpb):
            cols = [w[:, p] if q == p else jnp.zeros_like(w[:, p]) for q in range(hpb)]
            rows.append(jnp.concatenate(cols, axis=2))
        return jnp.concatenate(rows, axis=1)

    wgate = jnp.concatenate(
        [blockdiag(wa[0]), blockdiag(wi[0]), blockdiag(wa[1]), blockdiag(wi[1])], axis=2)

    def per_block(v):
        return v.reshape(ncb, 1, LANES)

    bgate = jnp.concatenate(
        [per_block(ba[0]), per_block(bi[0]), per_block(ba[1]), per_block(bi[1])], axis=2)
    lam_b = jnp.transpose(lam.reshape(2, ncb, LANES), (1, 0, 2))
    return wgate.astype(BF16), bgate, lam_b


def _shifted_rows(u, period):
    n = u.shape[0]
    col = lax.broadcasted_iota(jnp.int32, u.shape, 0) % period
    left = jnp.where(col == 0, 0.0, pltpu.roll(u, 1, 0))
    right = jnp.where(col == period - 1, 0.0, pltpu.roll(u, n - 1, 0))
    return left, right


def _out_kernel(latent, tiles_per_b, *refs):
    if latent:
        x_ref, cb_ref, u_ref, up_ref, un_ref, rec_ref, w_ref, cw_ref, mod_ref, o_ref = refs
    else:
        x_ref, cb_ref, u_ref, rec_ref, w_ref, cw_ref, mod_ref, o_ref = refs
    tm, dc = u_ref.shape
    cw = cw_ref[...]
    if latent:
        half = dc // 2
        uh = u_ref[:, 0:half].astype(F32)
        left, right = _shifted_rows(uh, GRID_W)
        conv_h = cw[0:1, 0:half] * left + cw[1:2, 0:half] * uh + cw[2:3, 0:half] * right
        uv = u_ref[:, half:dc].astype(F32)
        t = pl.program_id(0) % tiles_per_b
        prev = jnp.where(t > 0, up_ref[...].astype(F32), 0.0)
        nxt = jnp.where(t < tiles_per_b - 1, un_ref[...].astype(F32), 0.0)
        up = jnp.concatenate([prev, uv[0:tm - GRID_W]], axis=0)
        down = jnp.concatenate([uv[GRID_W:tm], nxt], axis=0)
        conv_v = cw[0:1, half:dc] * up + cw[1:2, half:dc] * uv + cw[2:3, half:dc] * down
        conv = jnp.concatenate([conv_h, conv_v], axis=1)
    else:
        u = u_ref[...].astype(F32)
        left, right = _shifted_rows(u, tm)
        conv = cw[0:1] * left + cw[1:2] * u + cw[2:3] * right
    conv_out = (cb_ref[...].astype(F32) * conv).astype(BF16)
    out = _dot(conv_out, w_ref[0:dc, :]) + _dot(rec_ref[...], w_ref[dc:, :])
    o_ref[...] = x_ref[...] + mod_ref[2:3, :] * out


def _out_call(xres, cb, u, rec, w_out, conv_w, mod_l, *, latent, tm, n_tiles, first_block,
              tiles_per_b, mod_index):
    d = xres.shape[1]
    dc = cb.shape[1]
    half = dc // 2
    tile = pl.BlockSpec((tm, dc), lambda i: (i, 0))
    xspec = pl.BlockSpec((tm, d), lambda i: (first_block + i, 0))
    in_specs = [xspec, tile, tile]
    args = [xres, cb, u]
    if latent:
        hb = tm // GRID_W
        last = n_tiles * hb - 1
        in_specs += [
            pl.BlockSpec((GRID_W, half), lambda i: (jnp.maximum(i * hb - 1, 0), 1)),
            pl.BlockSpec((GRID_W, half), lambda i: (jnp.minimum((i + 1) * hb, last), 1)),
        ]
        args += [u, u]
    in_specs += [
        tile,
        pl.BlockSpec(w_out.shape, lambda i: (0, 0)),
        pl.BlockSpec(conv_w.shape, lambda i: (0, 0)),
        pl.BlockSpec((None, 6, d), lambda i: (mod_index(i), 0, 0)),
    ]
    args += [rec, w_out, conv_w, mod_l]
    return pl.pallas_call(
        functools.partial(_out_kernel, latent, tiles_per_b),
        grid=(n_tiles,),
        in_specs=in_specs,
        out_specs=xspec,
        out_shape=jax.ShapeDtypeStruct(xres.shape, F32),
        input_output_aliases={0: 0},
        compiler_params=_cparams(1, 56),
        name="mix_out",
    )(*args)


def _ffn_kernel(fc, x_ref, mod_ref, g_ref, wg_ref, wu_ref, wd_ref, o_ref):
    x = x_ref[...]
    h = _norm_mod(x, g_ref[...], mod_ref[3:4, :], mod_ref[4:5, :]).astype(BF16)
    f = wg_ref.shape[1]
    acc = jnp.zeros(x.shape, F32)
    for f0 in range(0, f, fc):
        f1 = min(f0 + fc, f)
        gate = _dot(h, wg_ref[:, f0:f1])
        up = _dot(h, wu_ref[:, f0:f1])
        act = (gate * _sigmoid(gate) * up).astype(BF16)
        acc = acc + _dot(act, wd_ref[f0:f1, :])
    o_ref[...] = x + mod_ref[5:6, :] * acc


def _ffn_call(xres, mod_l, norm_g, wg, wu, wd, *, tm, mod_index):
    rows, d = xres.shape
    f = wg.shape[1]
    xspec = pl.BlockSpec((tm, d), lambda i: (i, 0))
    single = pl.Buffered(1)
    return pl.pallas_call(
        functools.partial(_ffn_kernel, 512),
        grid=(rows // tm,),
        in_specs=[
            xspec,
            pl.BlockSpec((None, 6, d), lambda i: (mod_index(i), 0, 0)),
            pl.BlockSpec((1, d), lambda i: (0, 0)),
            pl.BlockSpec((d, f), lambda i: (0, 0), pipeline_mode=single),
            pl.BlockSpec((d, f), lambda i: (0, 0), pipeline_mode=single),
            pl.BlockSpec((f, d), lambda i: (0, 0), pipeline_mode=single),
        ],
        out_specs=xspec,
        out_shape=jax.ShapeDtypeStruct(xres.shape, F32),
        input_output_aliases={0: 0},
        compiler_params=_cparams(1, 56),
        name="dense_ffn",
    )(xres, mod_l, norm_g, wg, wu, wd)


def _route(h, rw_ref, rb_ref, n_experts):
    logits = _dot3(h, rw_ref[...]) + rb_ref[...]
    lane = lax.broadcasted_iota(jnp.int32, logits.shape, 1)
    logits = jnp.where(lane < n_experts, logits, -jnp.inf)
    m1 = jnp.max(logits, axis=1, keepdims=True)
    i1 = jnp.min(jnp.where(logits == m1, lane, ROUTER_LANES), axis=1, keepdims=True)
    rest = jnp.where(lane == i1, -jnp.inf, logits)
    m2 = jnp.max(rest, axis=1, keepdims=True)
    i2 = jnp.min(jnp.where(rest == m2, lane, ROUTER_LANES), axis=1, keepdims=True)
    e2 = jnp.exp(m2 - m1)
    p1 = 1.0 / (1.0 + e2)
    p2 = e2 * p1
    return jnp.where(lane == i1, p1, 0.0) + jnp.where(lane == i2, p2, 0.0)


def _moe_dense_kernel(n_experts, x_ref, mod_ref, g_ref, rw_ref, rb_ref, wg_ref, wu_ref, wd_ref,
                      o_ref, h_ref, gates_ref, acc_ref):
    e = pl.program_id(1)
    k = pl.program_id(2)

    @pl.when((e == 0) & (k == 0))
    def _():
        h = _norm_mod(x_ref[...], g_ref[...], mod_ref[3:4, :], mod_ref[4:5, :])
        h_ref[...] = h.astype(BF16)
        gates_ref[...] = _route(h, rw_ref, rb_ref, n_experts)
        acc_ref[...] = jnp.zeros(acc_ref.shape, F32)

    gates = gates_ref[...]
    lane = lax.broadcasted_iota(jnp.int32, gates.shape, 1)
    ge = jnp.sum(jnp.where(lane == e, gates, 0.0), axis=1, keepdims=True)
    h = h_ref[...]
    gate = _dot(h, wg_ref[...])
    up = _dot(h, wu_ref[...])
    act = (ge * (gate * _sigmoid(gate) * up)).astype(BF16)
    acc_ref[...] += _dot(act, wd_ref[...])

    @pl.when((e == n_experts - 1) & (k == pl.num_programs(2) - 1))
    def _():
        o_ref[...] = x_ref[...] + mod_ref[5:6, :] * acc_ref[...]


def _moe_dense_call(xres, mod_l, norm_g, rw, rb, wg, wu, wd, *, tm, fc, mod_index):
    rows, d = xres.shape
    n_experts, _, f = wg.shape
    xspec = pl.BlockSpec((tm, d), lambda i, e, k: (i, 0))
    return pl.pallas_call(
        functools.partial(_moe_dense_kernel, n_experts),
        grid=(rows // tm, n_experts, f // fc),
        in_specs=[
            xspec,
            pl.BlockSpec((None, 6, d), lambda i, e, k: (mod_index(i), 0, 0)),
            pl.BlockSpec((1, d), lambda i, e, k: (0, 0)),
            pl.BlockSpec((d, ROUTER_LANES), lambda i, e, k: (0, 0)),
            pl.BlockSpec((1, ROUTER_LANES), lambda i, e, k: (0, 0)),
            pl.BlockSpec((None, d, fc), lambda i, e, k: (e, 0, k)),
            pl.BlockSpec((None, d, fc), lambda i, e, k: (e, 0, k)),
            pl.BlockSpec((None, fc, d), lambda i, e, k: (e, k, 0)),
        ],
        out_specs=xspec,
        out_shape=jax.ShapeDtypeStruct(xres.shape, F32),
        input_output_aliases={0: 0},
        scratch_shapes=[
            pltpu.VMEM((tm, d), BF16),
            pltpu.VMEM((tm, ROUTER_LANES), F32),
            pltpu.VMEM((tm, d), F32),
        ],
        compiler_params=_cparams(3, 56),
        name="moe_dense",
    )(xres, mod_l, norm_g, rw, rb, wg, wu, wd)


def _final_kernel(x_ref, g_ref, o_ref):
    x = x_ref[...]
    ms = jnp.mean(x * x, axis=-1, keepdims=True)
    o_ref[...] = x * lax.rsqrt(ms + EPS) * g_ref[...]


def _final_call(xres, g, *, rows, tm):
    d = xres.shape[1]
    spec = pl.BlockSpec((tm, d), lambda i: (i, 0))
    return pl.pallas_call(
        _final_kernel,
        grid=(rows // tm,),
        in_specs=[spec, pl.BlockSpec((1, d), lambda i: (0, 0))],
        out_specs=spec,
        out_shape=jax.ShapeDtypeStruct((rows, d), F32),
        compiler_params=_cparams(1, 48),
        name="final_norm",
    )(xres, g)


def kernel(x, c, ctx, c_ctx, ada_w, ada_b, norm_mix_g, norm_ffn_g, w_in, w_out, conv_w,
           rec_conv_w, rec_conv_b, lru_wa, lru_ba, lru_wi, lru_bi, lru_lam,
           dense_wg, dense_wu, dense_wd, router_w, router_b, moe_wg, moe_wu, moe_wd, final_g):
    batch, n_lat, d = x.shape
    n_ctx = ctx.shape[1]
    depth = ada_w.shape[0]
    n_experts = router_w.shape[2]
    rows_lat = batch * n_lat
    rows_ctx = batch * n_ctx
    tm = min(512, n_lat)
    tiles_per_b = n_lat // tm
    ctx_block0 = rows_lat // n_ctx

    xres = jnp.concatenate([x.reshape(rows_lat, d), ctx.reshape(rows_ctx, d)], axis=0)
    cvec = jnp.zeros((8, d), F32).at[:batch].set(c).at[batch].set(c_ctx)
    mod = _ada_call(cvec, ada_w, ada_b).reshape(depth, 8, 6, d)

    def lat_index(i):
        return i // tiles_per_b

    def ctx_index(i):
        return batch

    def all_index(i):
        return jnp.where(i < rows_lat // tm_all, (i * tm_all) // n_lat, batch)

    tm_all = min(512, n_ctx)
    rw = jnp.pad(router_w, ((0, 0), (0, 0), (0, ROUTER_LANES - n_experts)))
    rb = jnp.pad(router_b, ((0, 0), (0, ROUTER_LANES - n_experts)))[:, None, :]

    for l in range(depth):
        mod_l = mod[l]
        gmix = norm_mix_g[l][None, :]
        gffn = norm_ffn_g[l][None, :]
        w_in_l = w_in[l].astype(BF16)
        w_out_l = w_out[l].astype(BF16)

        cb_l, u_l, rx_l, g_l = _proj_call(xres, mod_l, gmix, w_in_l, tm=tm,
                                          n_tiles=rows_lat // tm, first_block=0,
                                          mod_index=lat_index)
        cb_c, u_c, rx_c, g_c = _proj_call(xres, mod_l, gmix, w_in_l, tm=n_ctx,
                                          n_tiles=batch, first_block=ctx_block0,
                                          mod_index=ctx_index)
        wgate, bgate, lam_b = _gate_params(lru_wa[l], lru_ba[l], lru_wi[l], lru_bi[l], lru_lam[l])
        rec_l, rec_c = _scan_call(rx_l, rx_c, g_l, g_c, rec_conv_w[l], rec_conv_b[l][None, :],
                                  wgate, bgate, lam_b, batch=batch)
        xres = _out_call(xres, cb_l, u_l, rec_l, w_out_l, conv_w[l], mod_l, latent=True, tm=tm,
                         n_tiles=rows_lat // tm, first_block=0, tiles_per_b=tiles_per_b,
                         mod_index=lat_index)
        if l < depth - 1:
            xres = _out_call(xres, cb_c, u_c, rec_c, w_out_l, conv_w[l], mod_l, latent=False,
                             tm=n_ctx, n_tiles=batch, first_block=ctx_block0, tiles_per_b=1,
                             mod_index=ctx_index)

        j = l // 2
        if l % 2 == 0:
            xres = _ffn_call(xres, mod_l, gffn, dense_wg[j].astype(BF16),
                             dense_wu[j].astype(BF16), dense_wd[j].astype(BF16),
                             tm=tm_all, mod_index=all_index)
        else:
            xres = _moe_dense_call(xres, mod_l, gffn, rw[j], rb[j], moe_wg[j].astype(BF16),
                                   moe_wu[j].astype(BF16), moe_wd[j].astype(BF16),
                                   tm=tm_all, fc=512, mod_index=all_index)

    out = _final_call(xres, final_g[None, :], rows=rows_lat, tm=tm)
    return out.reshape(batch, n_lat, d)
```

```python
import functools
import math

import jax
import jax.numpy as jnp
from jax import lax
from jax.experimental import pallas as pl
from jax.experimental.pallas import tpu as pltpu

GRID_W = 64
LRU_C = 8.0
EPS = 1e-6
REC_HEAD_DIM = 64

LANES = 128
SEG = 128
SEG_PITCH = SEG + 8
CTX_NSEG = 8
MAX_GROUP = 32
ROUTER_LANES = 128
ROUTE_BLOCK = 512
SEG_ALIGN = 16
EXPERT_ROWS = 1024
EXPERT_FC = 512

F32 = jnp.float32
BF16 = jnp.bfloat16


def _cparams(n_grid, vmem_mb):
    return pltpu.CompilerParams(
        dimension_semantics=("arbitrary",) * n_grid,
        vmem_limit_bytes=vmem_mb * 1024 * 1024)


def _dot(a, b):
    return jnp.dot(a, b, preferred_element_type=F32)


def _sigmoid(z):
    return 1.0 / (1.0 + jnp.exp(-z))


def _split_bf16(v):
    hi = v.astype(BF16)
    lo = (v - hi.astype(F32)).astype(BF16)
    return hi, lo


def _dot3(a, b):
    a_hi, a_lo = _split_bf16(a)
    b_hi, b_lo = _split_bf16(b)
    return _dot(a_hi, b_hi) + _dot(a_hi, b_lo) + _dot(a_lo, b_hi)


def _norm_mod(x, g, shift, scale):
    ms = jnp.mean(x * x, axis=-1, keepdims=True)
    y = x * lax.rsqrt(ms + EPS) * g
    return y * (1.0 + scale) + shift


def _ada_kernel(c_ref, w_ref, b_ref, o_ref):
    c = c_ref[...]
    s = c * _sigmoid(c)
    o_ref[...] = _dot3(s, w_ref[...]) + b_ref[...]


def _ada_call(cvec, ada_w, ada_b):
    depth, d, d6 = ada_w.shape
    tn = d6 // 4
    return pl.pallas_call(
        _ada_kernel,
        grid=(depth, d6 // tn),
        in_specs=[
            pl.BlockSpec((8, d), lambda l, j: (0, 0)),
            pl.BlockSpec((None, d, tn), lambda l, j: (l, 0, j)),
            pl.BlockSpec((None, 1, tn), lambda l, j: (l, 0, j)),
        ],
        out_specs=pl.BlockSpec((None, 8, tn), lambda l, j: (l, 0, j)),
        out_shape=jax.ShapeDtypeStruct((depth, 8, d6), F32),
        compiler_params=_cparams(2, 48),
        name="ada_mod",
    )(cvec, ada_w, ada_b.reshape(depth, 1, d6))


def _proj_kernel(dh, x_ref, mod_ref, g_ref, w_ref, cb_ref, u_ref, rx_ref, gl_ref):
    h = _norm_mod(x_ref[...], g_ref[...], mod_ref[0:1, :], mod_ref[1:2, :]).astype(BF16)

    def proj(j):
        return _dot(h, w_ref[:, j * dh:(j + 1) * dh])

    cb_ref[...] = proj(0).astype(BF16)
    u_ref[...] = (proj(1) * proj(2)).astype(BF16)
    rx_ref[...] = proj(3).astype(BF16)
    rg = proj(4)
    gelu = 0.5 * rg * (1.0 + jnp.tanh(math.sqrt(2.0 / math.pi) * (rg + 0.044715 * rg * rg * rg)))
    gl_ref[...] = gelu.astype(BF16)


def _proj_call(xres, mod_l, norm_g, w_in, *, tm, n_tiles, first_block, mod_index):
    d = xres.shape[1]
    d_in = w_in.shape[1]
    dh = d_in // 5
    out = jax.ShapeDtypeStruct((n_tiles * tm, dh), BF16)
    ospec = pl.BlockSpec((tm, dh), lambda i: (i, 0))
    return pl.pallas_call(
        functools.partial(_proj_kernel, dh),
        grid=(n_tiles,),
        in_specs=[
            pl.BlockSpec((tm, d), lambda i: (first_block + i, 0)),
            pl.BlockSpec((None, 6, d), lambda i: (mod_index(i), 0, 0)),
            pl.BlockSpec((1, d), lambda i: (0, 0)),
            pl.BlockSpec((d, d_in), lambda i: (0, 0)),
        ],
        out_specs=[ospec, ospec, ospec, ospec],
        out_shape=[out, out, out, out],
        compiler_params=_cparams(1, 56),
        name="mix_proj",
    )(xres, mod_l, norm_g, w_in)


def _scan_kernel(n_lat, n_ctx, rxl_ref, rxc_ref, gl_ref, gc_ref, cw_ref, cbias_ref, wg_ref,
                 gb_ref, lam_ref, recl_ref, recc_ref, xpl_ref, xpc_ref, ab_ref, cin_ref):
    nseg_l = n_lat // SEG
    seg_c = n_ctx // CTX_NSEG
    pitch_c = seg_c + 8
    base_l = CTX_NSEG * pitch_c
    group = min(MAX_GROUP, nseg_l)
    n_groups = nseg_l // group

    def stage(src_ref, dst_ref, n):
        dst_ref[0:8, :] = jnp.zeros((8, LANES), F32)
        dst_ref[n + 8:n + 16, :] = jnp.zeros((8, LANES), F32)
        rows = min(512, n)

        def body(k, _):
            r0 = pl.multiple_of(k * rows, rows)
            dst_ref[pl.ds(r0 + 8, rows), :] = src_ref[pl.ds(r0, rows), :].astype(F32)
            return 0

        lax.fori_loop(0, n // rows, body, 0)

    stage(rxl_ref, xpl_ref, n_lat)
    stage(rxc_ref, xpc_ref, n_ctx)

    lam = lam_ref[...]
    softplus = jnp.maximum(-lam, 0.0) + jnp.log1p(jnp.exp(-jnp.abs(lam)))
    cvec = -LRU_C * softplus
    cw = cw_ref[...]
    cbias = cbias_ref[...]
    gb = gb_ref[...]

    def gates(xp_ref, t0, nrows, dst):
        win = xp_ref[pl.ds(t0, nrows + 16), :]
        xr = (cw[0:1] * win[6:6 + nrows] + cw[1:2] * win[7:7 + nrows]
              + cw[2:3] * win[8:8 + nrows] + cw[3:4] * win[9:9 + nrows] + cbias)
        z = _dot(xr.astype(BF16), wg_ref[...]) + gb
        for d in range(2):
            r = _sigmoid(z[:, 2 * d * LANES:(2 * d + 1) * LANES])
            gi = _sigmoid(z[:, (2 * d + 1) * LANES:(2 * d + 2) * LANES])
            a = jnp.exp(cvec[d:d + 1] * r)
            b = jnp.sqrt((1.0 - a) * (1.0 + a)) * (gi * xr)
            ab_ref[2 * d, pl.ds(dst, nrows), :] = a
            ab_ref[2 * d + 1, pl.ds(dst, nrows), :] = b

    def gates_ctx(j, _):
        gates(xpc_ref, pl.multiple_of(j * seg_c, 8), seg_c, pl.multiple_of(j * pitch_c, 8))
        return 0

    def gates_lat(j, _):
        gates(xpl_ref, pl.multiple_of(j * SEG, SEG), SEG,
              pl.multiple_of(base_l + j * SEG_PITCH, 8))
        return 0

    lax.fori_loop(0, CTX_NSEG, gates_ctx, 0)
    lax.fori_loop(0, nseg_l, gates_lat, 0)

    def scan_group(d, base, nseg, seg_len, pitch, seg0, carry, reverse):
        a_ref = ab_ref.at[2 * d]
        b_ref = ab_ref.at[2 * d + 1]

        def step(k, hc):
            h, acc = hc
            s = (seg_len - 1 - k) if reverse else k
            idx = pl.ds(base + s, nseg, stride=pitch)
            a = a_ref[idx, :]
            b = b_ref[idx, :]
            h = a * h + b
            acc = a * acc
            b_ref[idx, :] = h
            a_ref[idx, :] = acc
            return h, acc

        h_end, a_end = lax.fori_loop(
            0, seg_len, step, (jnp.zeros((nseg, LANES), F32), jnp.ones((nseg, LANES), F32)))
        order = range(nseg - 1, -1, -1) if reverse else range(nseg)
        for j in order:
            cin_ref[d, seg0 + j:seg0 + j + 1, :] = carry
            carry = h_end[j:j + 1] + a_end[j:j + 1] * carry
        return carry

    for d, reverse in ((0, False), (1, True)):
        carry = jnp.zeros((1, LANES), F32)
        carry = scan_group(d, 0, CTX_NSEG, seg_c, pitch_c, 0, carry, reverse)
        groups = range(n_groups - 1, -1, -1) if reverse else range(n_groups)
        for gidx in groups:
            carry = scan_group(d, base_l + gidx * group * SEG_PITCH, group, SEG, SEG_PITCH,
                               CTX_NSEG + gidx * group, carry, reverse)

    def emit(g_ref, out_ref, t0, nrows, src, seg):
        c0 = cin_ref[0, pl.ds(seg, 1), :]
        c1 = cin_ref[1, pl.ds(seg, 1), :]
        h = (ab_ref[1, pl.ds(src, nrows), :] + ab_ref[0, pl.ds(src, nrows), :] * c0
             + ab_ref[3, pl.ds(src, nrows), :] + ab_ref[2, pl.ds(src, nrows), :] * c1)
        g = g_ref[pl.ds(t0, nrows), :].astype(F32)
        out_ref[pl.ds(t0, nrows), :] = (h * g).astype(BF16)

    def emit_ctx(j, _):
        emit(gc_ref, recc_ref, pl.multiple_of(j * seg_c, 8), seg_c,
             pl.multiple_of(j * pitch_c, 8), j)
        return 0

    def emit_lat(j, _):
        emit(gl_ref, recl_ref, pl.multiple_of(j * SEG, SEG), SEG,
             pl.multiple_of(base_l + j * SEG_PITCH, 8), CTX_NSEG + j)
        return 0

    lax.fori_loop(0, CTX_NSEG, emit_ctx, 0)
    lax.fori_loop(0, nseg_l, emit_lat, 0)


def _scan_call(rx_l, rx_c, g_l, g_c, conv_w, conv_b, wgate, bgate, lam, *, batch):
    n_lat = rx_l.shape[0] // batch
    n_ctx = rx_c.shape[0] // batch
    d_rec = rx_l.shape[1]
    ncb = d_rec // LANES
    seg_c = n_ctx // CTX_NSEG
    ab_rows = CTX_NSEG * (seg_c + 8) + (n_lat // SEG) * SEG_PITCH
    nseg_tot = CTX_NSEG + n_lat // SEG
    lat = pl.BlockSpec((n_lat, LANES), lambda b, c: (b, c))
    ctx = pl.BlockSpec((n_ctx, LANES), lambda b, c: (b, c))
    return pl.pallas_call(
        functools.partial(_scan_kernel, n_lat, n_ctx),
        grid=(batch, ncb),
        in_specs=[
            lat, ctx, lat, ctx,
            pl.BlockSpec((4, LANES), lambda b, c: (0, c)),
            pl.BlockSpec((1, LANES), lambda b, c: (0, c)),
            pl.BlockSpec((None, LANES, 4 * LANES), lambda b, c: (c, 0, 0)),
            pl.BlockSpec((None, 1, 4 * LANES), lambda b, c: (c, 0, 0)),
            pl.BlockSpec((None, 2, LANES), lambda b, c: (c, 0, 0)),
        ],
        out_specs=[lat, ctx],
        out_shape=[jax.ShapeDtypeStruct(rx_l.shape, BF16),
                   jax.ShapeDtypeStruct(rx_c.shape, BF16)],
        scratch_shapes=[
            pltpu.VMEM((n_lat + 16, LANES), F32),
            pltpu.VMEM((n_ctx + 16, LANES), F32),
            pltpu.VMEM((4, ab_rows, LANES), F32),
            pltpu.VMEM((2, nseg_tot, LANES), F32),
        ],
        compiler_params=_cparams(2, 56),
        name="rglru_scan",
    )(rx_l, rx_c, g_l, g_c, conv_w, conv_b, wgate, bgate, lam)


def _gate_params(wa, ba, wi, bi, lam):
    n_heads = wa.shape[1]
    hpb = LANES // REC_HEAD_DIM
    ncb = n_heads // hpb

    def blockdiag(w):
        w = w.reshape(ncb, hpb, REC_HEAD_DIM, REC_HEAD_DIM)
        rows = []
        for p in range(hpb):
            cols = [w[:, p] if q == p else jnp.zeros_like(w[:, p]) for q in range(hpb)]
            rows.append(jnp.concatenate(cols, axis=2))
        return jnp.concatenate(rows, axis=1)

    wgate = jnp.concatenate(
        [blockdiag(wa[0]), blockdiag(wi[0]), blockdiag(wa[1]), blockdiag(wi[1])], axis=2)

    def per_block(v):
        return v.reshape(ncb, 1, LANES)

    bgate = jnp.concatenate(
        [per_block(ba[0]), per_block(bi[0]), per_block(ba[1]), per_block(bi[1])], axis=2)
    lam_b = jnp.transpose(lam.reshape(2, ncb, LANES), (1, 0, 2))
    return wgate.astype(BF16), bgate, lam_b


def _shifted_rows(u, period):
    n = u.shape[0]
    col = lax.broadcasted_iota(jnp.int32, u.shape, 0) % period
    left = jnp.where(col == 0, 0.0, pltpu.roll(u, 1, 0))
    right = jnp.where(col == period - 1, 0.0, pltpu.roll(u, n - 1, 0))
    return left, right


def _out_kernel(latent, tiles_per_b, *refs):
    if latent:
        x_ref, cb_ref, u_ref, up_ref, un_ref, rec_ref, w_ref, cw_ref, mod_ref, o_ref = refs
    else:
        x_ref, cb_ref, u_ref, rec_ref, w_ref, cw_ref, mod_ref, o_ref = refs
    tm, dc = u_ref.shape
    cw = cw_ref[...]
    if latent:
        half = dc // 2
        uh = u_ref[:, 0:half].astype(F32)
        left, right = _shifted_rows(uh, GRID_W)
        conv_h = cw[0:1, 0:half] * left + cw[1:2, 0:half] * uh + cw[2:3, 0:half] * right
        uv = u_ref[:, half:dc].astype(F32)
        t = pl.program_id(0) % tiles_per_b
        prev = jnp.where(t > 0, up_ref[...].astype(F32), 0.0)
        nxt = jnp.where(t < tiles_per_b - 1, un_ref[...].astype(F32), 0.0)
        up = jnp.concatenate([prev, uv[0:tm - GRID_W]], axis=0)
        down = jnp.concatenate([uv[GRID_W:tm], nxt], axis=0)
        conv_v = cw[0:1, half:dc] * up + cw[1:2, half:dc] * uv + cw[2:3, half:dc] * down
        conv = jnp.concatenate([conv_h, conv_v], axis=1)
    else:
        u = u_ref[...].astype(F32)
        left, right = _shifted_rows(u, tm)
        conv = cw[0:1] * left + cw[1:2] * u + cw[2:3] * right
    conv_out = (cb_ref[...].astype(F32) * conv).astype(BF16)
    out = _dot(conv_out, w_ref[0:dc, :]) + _dot(rec_ref[...], w_ref[dc:, :])
    o_ref[...] = x_ref[...] + mod_ref[2:3, :] * out


def _out_call(xres, cb, u, rec, w_out, conv_w, mod_l, *, latent, tm, n_tiles, first_block,
              tiles_per_b, mod_index):
    d = xres.shape[1]
    dc = cb.shape[1]
    half = dc // 2
    tile = pl.BlockSpec((tm, dc), lambda i: (i, 0))
    xspec = pl.BlockSpec((tm, d), lambda i: (first_block + i, 0))
    in_specs = [xspec, tile, tile]
    args = [xres, cb, u]
    if latent:
        hb = tm // GRID_W
        last = n_tiles * hb - 1
        in_specs += [
            pl.BlockSpec((GRID_W, half), lambda i: (jnp.maximum(i * hb - 1, 0), 1)),
            pl.BlockSpec((GRID_W, half), lambda i: (jnp.minimum((i + 1) * hb, last), 1)),
        ]
        args += [u, u]
    in_specs += [
        tile,
        pl.BlockSpec(w_out.shape, lambda i: (0, 0)),
        pl.BlockSpec(conv_w.shape, lambda i: (0, 0)),
        pl.BlockSpec((None, 6, d), lambda i: (mod_index(i), 0, 0)),
    ]
    args += [rec, w_out, conv_w, mod_l]
    return pl.pallas_call(
        functools.partial(_out_kernel, latent, tiles_per_b),
        grid=(n_tiles,),
        in_specs=in_specs,
        out_specs=xspec,
        out_shape=jax.ShapeDtypeStruct(xres.shape, F32),
        input_output_aliases={0: 0},
        compiler_params=_cparams(1, 56),
        name="mix_out",
    )(*args)


def _ffn_kernel(fc, x_ref, mod_ref, g_ref, wg_ref, wu_ref, wd_ref, o_ref):
    x = x_ref[...]
    h = _norm_mod(x, g_ref[...], mod_ref[3:4, :], mod_ref[4:5, :]).astype(BF16)
    f = wg_ref.shape[1]
    acc = jnp.zeros(x.shape, F32)
    for f0 in range(0, f, fc):
        f1 = min(f0 + fc, f)
        gate = _dot(h, wg_ref[:, f0:f1])
        up = _dot(h, wu_ref[:, f0:f1])
        act = (gate * _sigmoid(gate) * up).astype(BF16)
        acc = acc + _dot(act, wd_ref[f0:f1, :])
    o_ref[...] = x + mod_ref[5:6, :] * acc


def _ffn_call(xres, mod_l, norm_g, wg, wu, wd, *, tm, mod_index):
    rows, d = xres.shape
    f = wg.shape[1]
    xspec = pl.BlockSpec((tm, d), lambda i: (i, 0))
    single = pl.Buffered(1)
    return pl.pallas_call(
        functools.partial(_ffn_kernel, 512),
        grid=(rows // tm,),
        in_specs=[
            xspec,
            pl.BlockSpec((None, 6, d), lambda i: (mod_index(i), 0, 0)),
            pl.BlockSpec((1, d), lambda i: (0, 0)),
            pl.BlockSpec((d, f), lambda i: (0, 0), pipeline_mode=single),
            pl.BlockSpec((d, f), lambda i: (0, 0), pipeline_mode=single),
            pl.BlockSpec((f, d), lambda i: (0, 0), pipeline_mode=single),
        ],
        out_specs=xspec,
        out_shape=jax.ShapeDtypeStruct(xres.shape, F32),
        input_output_aliases={0: 0},
        compiler_params=_cparams(1, 56),
        name="dense_ffn",
    )(xres, mod_l, norm_g, wg, wu, wd)


def _route_kernel(n_experts, x_ref, mod_ref, g_ref, rw_ref, rb_ref,
                  hb_ref, info_ref, infot_ref, cnt_ref):
    tb = x_ref.shape[0]
    h = _norm_mod(x_ref[...], g_ref[...], mod_ref[3:4, :], mod_ref[4:5, :])
    hb_ref[...] = h.astype(BF16)
    logits = _dot3(h, rw_ref[...]) + rb_ref[...]
    lane = lax.broadcasted_iota(jnp.int32, logits.shape, 1)
    logits = jnp.where(lane < n_experts, logits, -jnp.inf)
    m1 = jnp.max(logits, axis=1, keepdims=True)
    i1 = jnp.min(jnp.where(logits == m1, lane, ROUTER_LANES), axis=1, keepdims=True)
    rest = jnp.where(lane == i1, -jnp.inf, logits)
    m2 = jnp.max(rest, axis=1, keepdims=True)
    i2 = jnp.min(jnp.where(rest == m2, lane, ROUTER_LANES), axis=1, keepdims=True)
    e2 = jnp.exp(m2 - m1)
    p1 = 1.0 / (1.0 + e2)
    p2 = e2 * p1

    member = jnp.where((lane == i1) | (lane == i2), 1.0, 0.0)
    earlier = jnp.where(lax.broadcasted_iota(jnp.int32, (tb, tb), 0)
                        > lax.broadcasted_iota(jnp.int32, (tb, tb), 1), 1.0, 0.0)
    rank = _dot(earlier.astype(BF16), member.astype(BF16))
    cnt = jnp.sum(member, axis=0, keepdims=True)
    units = jnp.floor((cnt + (SEG_ALIGN - 1.0)) * (1.0 / SEG_ALIGN))
    before = jnp.where(lax.broadcasted_iota(jnp.int32, (ROUTER_LANES, ROUTER_LANES), 0)
                       < lax.broadcasted_iota(jnp.int32, (ROUTER_LANES, ROUTER_LANES), 1), 1.0, 0.0)
    start = _dot(jnp.broadcast_to(units, (8, ROUTER_LANES)).astype(BF16),
                 before.astype(BF16))[0:1, :] * float(SEG_ALIGN)
    pos = start + rank
    dest1 = jnp.sum(jnp.where(lane == i1, pos, 0.0), axis=1, keepdims=True)
    dest2 = jnp.sum(jnp.where(lane == i2, pos, 0.0), axis=1, keepdims=True)
    info = (jnp.where(lane == 0, dest1, 0.0) + jnp.where(lane == 1, dest2, 0.0)
            + jnp.where(lane == 2, p1, 0.0) + jnp.where(lane == 3, p2, 0.0))
    info_ref[...] = info
    infot_ref[...] = info.T[0:8, :]
    cnt_ref[...] = jnp.broadcast_to(cnt, (8, ROUTER_LANES))


def _route_call(xres, mod_l, norm_g, rw, rb, *, n_blocks, n_experts, mod_index):
    d = xres.shape[1]
    tb = ROUTE_BLOCK
    rows = n_blocks * tb
    return pl.pallas_call(
        functools.partial(_route_kernel, n_experts),
        grid=(n_blocks,),
        in_specs=[
            pl.BlockSpec((tb, d), lambda i: (i, 0)),
            pl.BlockSpec((None, 6, d), lambda i: (mod_index(i), 0, 0)),
            pl.BlockSpec((1, d), lambda i: (0, 0)),
            pl.BlockSpec((d, ROUTER_LANES), lambda i: (0, 0)),
            pl.BlockSpec((1, ROUTER_LANES), lambda i: (0, 0)),
        ],
        out_specs=[
            pl.BlockSpec((tb, d), lambda i: (i, 0)),
            pl.BlockSpec((tb, ROUTER_LANES), lambda i: (i, 0)),
            pl.BlockSpec((None, 8, tb), lambda i: (i, 0, 0)),
            pl.BlockSpec((None, 8, ROUTER_LANES), lambda i: (i, 0, 0)),
        ],
        out_shape=[
            jax.ShapeDtypeStruct((rows, d), BF16),
            jax.ShapeDtypeStruct((rows, ROUTER_LANES), F32),
            jax.ShapeDtypeStruct((n_blocks, 8, tb), F32),
            jax.ShapeDtypeStruct((n_blocks, 8, ROUTER_LANES), F32),
        ],
        compiler_params=_cparams(1, 48),
        name="moe_route",
    )(xres, mod_l, norm_g, rw, rb)


def _segment_copies(n_experts, b, cntp_sm, loff_sm, dst_sm, local_ref, global_ref, sem, to_global):
    copies = []
    for e in range(n_experts):
        cp = cntp_sm[b * n_experts + e]
        lo = loff_sm[b * n_experts + e]
        go = dst_sm[b * n_experts + e]
        pos = jnp.int32(0)
        size = ROUTE_BLOCK
        while size >= SEG_ALIGN:
            bit = (cp & size) != 0
            loc = local_ref.at[pl.ds(pl.multiple_of(lo + pos, SEG_ALIGN), size)]
            glo = global_ref.at[pl.ds(pl.multiple_of(go + pos, SEG_ALIGN), size)]
            cpy = (pltpu.make_async_copy(loc, glo, sem) if to_global
                   else pltpu.make_async_copy(glo, loc, sem))
            copies.append((bit, cpy))
            pos = pos + jnp.where(bit, size, 0)
            size //= 2
    return copies


def _sort_kernel(n_experts, cntp_sm, loff_sm, dst_sm, hb_ref, infot_ref, zero_ref, sorted_ref,
                 tile_ref, sem):
    del zero_ref
    b = pl.program_id(0)
    s_loc, tb = tile_ref.shape[0], hb_ref.shape[0]
    d1 = infot_ref[0:1, :].astype(jnp.int32)
    d2 = infot_ref[1:2, :].astype(jnp.int32)
    r = lax.broadcasted_iota(jnp.int32, (s_loc, tb), 0)
    perm = jnp.where((r == d1) | (r == d2), 1.0, 0.0).astype(BF16)
    tile_ref[...] = _dot(perm, hb_ref[...]).astype(BF16)
    copies = _segment_copies(n_experts, b, cntp_sm, loff_sm, dst_sm, tile_ref, sorted_ref, sem, True)
    for bit, cpy in copies:
        pl.when(bit)(cpy.start)
    for bit, cpy in copies:
        pl.when(bit)(cpy.wait)


def _sort_call(cntp, loff, dst, hb, infot, sorted_init, *, n_blocks, n_experts, s_loc):
    d = hb.shape[1]
    tb = ROUTE_BLOCK
    return pl.pallas_call(
        functools.partial(_sort_kernel, n_experts),
        grid_spec=pltpu.PrefetchScalarGridSpec(
            num_scalar_prefetch=3,
            grid=(n_blocks,),
            in_specs=[
                pl.BlockSpec((tb, d), lambda i, *_: (i, 0)),
                pl.BlockSpec((None, 8, tb), lambda i, *_: (i, 0, 0)),
                pl.BlockSpec(memory_space=pl.ANY),
            ],
            out_specs=pl.BlockSpec(memory_space=pl.ANY),
            scratch_shapes=[pltpu.VMEM((s_loc, d), BF16), pltpu.SemaphoreType.DMA(())],
        ),
        out_shape=jax.ShapeDtypeStruct(sorted_init.shape, BF16),
        input_output_aliases={5: 0},
        compiler_params=_cparams(1, 48),
        name="moe_sort",
    )(cntp, loff, dst, hb, infot, sorted_init)


def _expert_kernel(fc, te_sm, valid_sm, xs_ref, wgu_ref, wd_ref, o_ref, acc_ref):
    del te_sm
    i = pl.program_id(0)
    k = pl.program_id(1)
    last = pl.num_programs(1) - 1
    valid = valid_sm[i] != 0

    @pl.when(valid)
    def _():
        gu = _dot(xs_ref[...], wgu_ref[...])
        gate = gu[:, 0:fc]
        up = gu[:, fc:2 * fc]
        act = (gate * _sigmoid(gate) * up).astype(BF16)
        y = _dot(act, wd_ref[...])

        @pl.when(k == 0)
        def _():
            acc_ref[...] = y

        @pl.when(k > 0)
        def _():
            acc_ref[...] += y

        @pl.when(k == last)
        def _():
            o_ref[...] = acc_ref[...].astype(BF16)

    @pl.when(jnp.logical_not(valid) & (k == last))
    def _():
        o_ref[...] = jnp.zeros(o_ref.shape, BF16)


def _expert_call(tile_expert, tile_valid, xs, wgu, wd, *, tmx):
    rows, d = xs.shape
    n_k, fc2 = wgu.shape[1], wgu.shape[3]
    fc = fc2 // 2

    def kk(i, k, te, tv):
        return jnp.where(tv[i] != 0, k, n_k - 1)

    return pl.pallas_call(
        functools.partial(_expert_kernel, fc),
        grid_spec=pltpu.PrefetchScalarGridSpec(
            num_scalar_prefetch=2,
            grid=(rows // tmx, n_k),
            in_specs=[
                pl.BlockSpec((tmx, d), lambda i, k, te, tv: (i, 0)),
                pl.BlockSpec((None, None, d, fc2), lambda i, k, te, tv: (te[i], kk(i, k, te, tv), 0, 0)),
                pl.BlockSpec((None, fc, d), lambda i, k, te, tv: (te[i], kk(i, k, te, tv), 0)),
            ],
            out_specs=pl.BlockSpec((tmx, d), lambda i, k, te, tv: (i, 0)),
            scratch_shapes=[pltpu.VMEM((tmx, d), F32)],
        ),
        out_shape=jax.ShapeDtypeStruct((rows, d), BF16),
        compiler_params=_cparams(2, 56),
        name="moe_experts",
    )(tile_expert, tile_valid, xs, wgu, wd)


def _combine_kernel(n_experts, final, cntp_sm, loff_sm, dst_sm, x_ref, info_ref, mod_ref, *rest):
    if final:
        fg_ref, ys_ref, o_ref, tile_ref, sem = rest
    else:
        ys_ref, o_ref, tile_ref, sem = rest
    b = pl.program_id(0)
    s_loc, tb = tile_ref.shape[0], x_ref.shape[0]
    tile_ref[...] = jnp.zeros(tile_ref.shape, BF16)
    copies = _segment_copies(n_experts, b, cntp_sm, loff_sm, dst_sm, tile_ref, ys_ref, sem, False)
    for bit, cpy in copies:
        pl.when(bit)(cpy.start)
    info = info_ref[...]
    d1 = info[:, 0:1].astype(jnp.int32)
    d2 = info[:, 1:2].astype(jnp.int32)
    c = lax.broadcasted_iota(jnp.int32, (tb, s_loc), 1)
    pg = (jnp.where(c == d1, info[:, 2:3], 0.0) + jnp.where(c == d2, info[:, 3:4], 0.0)).astype(BF16)
    for bit, cpy in copies:
        pl.when(bit)(cpy.wait)
    y = _dot(pg, tile_ref[...])
    xn = x_ref[...] + mod_ref[5:6, :] * y
    if final:
        ms = jnp.mean(xn * xn, axis=-1, keepdims=True)
        xn = xn * lax.rsqrt(ms + EPS) * fg_ref[...]
    o_ref[...] = xn


def _combine_call(cntp, loff, dst, xres, info, mod_l, ys, final_g, *, n_blocks, n_experts,
                  s_loc, mod_index):
    d = xres.shape[1]
    tb = ROUTE_BLOCK
    final = final_g is not None
    xspec = pl.BlockSpec((tb, d), lambda i, *_: (i, 0))
    in_specs = [
        xspec,
        pl.BlockSpec((tb, ROUTER_LANES), lambda i, *_: (i, 0)),
        pl.BlockSpec((None, 6, d), lambda i, *_: (mod_index(i), 0, 0)),
    ]
    args = [xres, info, mod_l]
    if final:
        in_specs.append(pl.BlockSpec((1, d), lambda i, *_: (0, 0)))
        args.append(final_g)
    in_specs.append(pl.BlockSpec(memory_space=pl.ANY))
    args.append(ys)
    out_rows = n_blocks * tb if final else xres.shape[0]
    return pl.pallas_call(
        functools.partial(_combine_kernel, n_experts, final),
        grid_spec=pltpu.PrefetchScalarGridSpec(
            num_scalar_prefetch=3,
            grid=(n_blocks,),
            in_specs=in_specs,
            out_specs=xspec,
            scratch_shapes=[pltpu.VMEM((s_loc, d), BF16), pltpu.SemaphoreType.DMA(())],
        ),
        out_shape=jax.ShapeDtypeStruct((out_rows, d), F32),
        input_output_aliases={} if final else {3: 0},
        compiler_params=_cparams(1, 48),
        name="moe_combine",
    )(cntp, loff, dst, *args)


def _moe_layout(cnt, n_tiles, tmx):
    cntp = (cnt + (SEG_ALIGN - 1)) // SEG_ALIGN * SEG_ALIGN
    loff = jnp.cumsum(cntp, axis=1) - cntp
    total = jnp.sum(cntp, axis=0)
    region = (total + (tmx - 1)) // tmx * tmx
    region_end = jnp.cumsum(region)
    dst = (region_end - region)[None, :] + jnp.cumsum(cntp, axis=0) - cntp
    tile_start = jnp.arange(n_tiles, dtype=jnp.int32) * tmx
    valid = tile_start < region_end[-1]
    last_tile = jnp.maximum(region_end[-1] // tmx - 1, 0) * tmx
    owner = jnp.searchsorted(region_end, jnp.where(valid, tile_start, last_tile), side="right")
    flat = lambda v: v.reshape(-1).astype(jnp.int32)
    return flat(cntp), flat(loff), flat(dst), owner.astype(jnp.int32), valid.astype(jnp.int32)


def _moe_call(xres, mod_l, norm_g, rw, rb, wgu, wd, final_g, *, n_blocks, mod_index):
    d = xres.shape[1]
    n_experts = wgu.shape[0]
    tmx = EXPERT_ROWS
    block_rows = 2 * ROUTE_BLOCK + n_experts * (SEG_ALIGN - 1)
    s_loc = -(-block_rows // LANES) * LANES
    max_rows = n_blocks * block_rows + n_experts * (tmx - SEG_ALIGN)
    n_tiles = max_rows // tmx
    hb, info, infot, cnt = _route_call(xres, mod_l, norm_g, rw, rb, n_blocks=n_blocks,
                                       n_experts=n_experts, mod_index=mod_index)
    cnt = cnt[:, 0, :n_experts].astype(jnp.int32)
    cntp, loff, dst, owner, valid = _moe_layout(cnt, n_tiles, tmx)
    xs = _sort_call(cntp, loff, dst, hb, infot, jnp.zeros((n_tiles * tmx, d), BF16),
                    n_blocks=n_blocks, n_experts=n_experts, s_loc=s_loc)
    ys = _expert_call(owner, valid, xs, wgu, wd, tmx=tmx)
    return _combine_call(cntp, loff, dst, xres, info, mod_l, ys, final_g, n_blocks=n_blocks,
                         n_experts=n_experts, s_loc=s_loc, mod_index=mod_index)


def _pack_expert_weights(wg, wu, wd, fc):
    n_e, d, f = wg.shape

    def chunks(w):
        return jnp.transpose(w.reshape(n_e, d, f // fc, fc), (0, 2, 1, 3))

    wgu = jnp.concatenate([chunks(wg), chunks(wu)], axis=3).astype(BF16)
    return wgu, wd.astype(BF16)


def _final_kernel(x_ref, g_ref, o_ref):
    x = x_ref[...]
    ms = jnp.mean(x * x, axis=-1, keepdims=True)
    o_ref[...] = x * lax.rsqrt(ms + EPS) * g_ref[...]


def _final_call(xres, g, *, rows, tm):
    d = xres.shape[1]
    spec = pl.BlockSpec((tm, d), lambda i: (i, 0))
    return pl.pallas_call(
        _final_kernel,
        grid=(rows // tm,),
        in_specs=[spec, pl.BlockSpec((1, d), lambda i: (0, 0))],
        out_specs=spec,
        out_shape=jax.ShapeDtypeStruct((rows, d), F32),
        compiler_params=_cparams(1, 48),
        name="final_norm",
    )(xres, g)


def kernel(x, c, ctx, c_ctx, ada_w, ada_b, norm_mix_g, norm_ffn_g, w_in, w_out, conv_w,
           rec_conv_w, rec_conv_b, lru_wa, lru_ba, lru_wi, lru_bi, lru_lam,
           dense_wg, dense_wu, dense_wd, router_w, router_b, moe_wg, moe_wu, moe_wd, final_g):
    batch, n_lat, d = x.shape
    n_ctx = ctx.shape[1]
    depth = ada_w.shape[0]
    n_experts = router_w.shape[2]
    rows_lat = batch * n_lat
    rows_ctx = batch * n_ctx
    tm = min(512, n_lat)
    tiles_per_b = n_lat // tm
    ctx_block0 = rows_lat // n_ctx
    tb = ROUTE_BLOCK
    assert n_lat % tb == 0 and rows_ctx % tb == 0 and batch < 8

    xres = jnp.concatenate([x.reshape(rows_lat, d), ctx.reshape(rows_ctx, d)], axis=0)
    cvec = jnp.zeros((8, d), F32).at[:batch].set(c).at[batch].set(c_ctx)
    mod = _ada_call(cvec, ada_w, ada_b).reshape(depth, 8, 6, d)

    def lat_index(i):
        return i // tiles_per_b

    def ctx_index(i):
        return batch

    def all_index(i):
        return jnp.where(i < rows_lat // tb, (i * tb) // n_lat, batch)

    rw = jnp.pad(router_w, ((0, 0), (0, 0), (0, ROUTER_LANES - n_experts)))
    rb = jnp.pad(router_b, ((0, 0), (0, ROUTER_LANES - n_experts)))[:, None, :]

    out = None
    for l in range(depth):
        last = l == depth - 1
        mod_l = mod[l]
        gmix = norm_mix_g[l][None, :]
        gffn = norm_ffn_g[l][None, :]
        w_in_l = w_in[l].astype(BF16)
        w_out_l = w_out[l].astype(BF16)

        cb_l, u_l, rx_l, g_l = _proj_call(xres, mod_l, gmix, w_in_l, tm=tm,
                                          n_tiles=rows_lat // tm, first_block=0,
                                          mod_index=lat_index)
        cb_c, u_c, rx_c, g_c = _proj_call(xres, mod_l, gmix, w_in_l, tm=n_ctx,
                                          n_tiles=batch, first_block=ctx_block0,
                                          mod_index=ctx_index)
        wgate, bgate, lam_b = _gate_params(lru_wa[l], lru_ba[l], lru_wi[l], lru_bi[l], lru_lam[l])
        rec_l, rec_c = _scan_call(rx_l, rx_c, g_l, g_c, rec_conv_w[l], rec_conv_b[l][None, :],
                                  wgate, bgate, lam_b, batch=batch)
        xres = _out_call(xres, cb_l, u_l, rec_l, w_out_l, conv_w[l], mod_l, latent=True, tm=tm,
                         n_tiles=rows_lat // tm, first_block=0, tiles_per_b=tiles_per_b,
                         mod_index=lat_index)
        if not last:
            xres = _out_call(xres, cb_c, u_c, rec_c, w_out_l, conv_w[l], mod_l, latent=False,
                             tm=n_ctx, n_tiles=batch, first_block=ctx_block0, tiles_per_b=1,
                             mod_index=ctx_index)

        j = l // 2
        if l % 2 == 0:
            xres = _ffn_call(xres, mod_l, gffn, dense_wg[j].astype(BF16),
                             dense_wu[j].astype(BF16), dense_wd[j].astype(BF16),
                             tm=tb, mod_index=all_index)
        else:
            wgu, wd = _pack_expert_weights(moe_wg[j], moe_wu[j], moe_wd[j], EXPERT_FC)
            n_blocks = rows_lat // tb if last else (rows_lat + rows_ctx) // tb
            res = _moe_call(xres, mod_l, gffn, rw[j], rb[j], wgu, wd,
                            final_g[None, :] if last else None, n_blocks=n_blocks,
                            mod_index=all_index)
            if last:
                out = res
            else:
                xres = res

    if out is None:
        out = _final_call(xres, final_g[None, :], rows=rows_lat, tm=tm)
    return out.reshape(batch, n_lat, d)
```

```python
import functools
import math

import jax
import jax.numpy as jnp
from jax import lax
from jax.experimental import pallas as pl
from jax.experimental.pallas import tpu as pltpu

GRID_W = 64
LRU_C = 8.0
EPS = 1e-6
REC_HEAD_DIM = 64

LANES = 128
SEG = 128
SEG_PITCH = SEG + 8
CTX_NSEG = 8
MAX_GROUP = 64
GATE_SEGS = 4
SCAN_UNROLL = 8
ROUTER_LANES = 128
ROUTE_BLOCK = 512
SEG_ALIGN = 16
EXPERT_ROWS = 1024
EXPERT_FC = 512

F32 = jnp.float32
BF16 = jnp.bfloat16


def _cparams(n_grid, vmem_mb):
    return pltpu.CompilerParams(
        dimension_semantics=("arbitrary",) * n_grid,
        vmem_limit_bytes=vmem_mb * 1024 * 1024)


def _dot(a, b):
    return jnp.dot(a, b, preferred_element_type=F32)


def _sigmoid(z):
    return 1.0 / (1.0 + jnp.exp2(z * (-math.log2(math.e))))


def _split_bf16(v):
    hi = v.astype(BF16)
    lo = (v - hi.astype(F32)).astype(BF16)
    return hi, lo


def _dot3(a, b):
    a_hi, a_lo = _split_bf16(a)
    b_hi, b_lo = _split_bf16(b)
    return _dot(a_hi, b_hi) + _dot(a_hi, b_lo) + _dot(a_lo, b_hi)


def _norm_mod(x, g, shift, scale):
    ms = jnp.mean(x * x, axis=-1, keepdims=True)
    y = x * lax.rsqrt(ms + EPS) * g
    return y * (1.0 + scale) + shift


def _ada_kernel(c_ref, w_ref, b_ref, o_ref):
    c = c_ref[...]
    s = c * _sigmoid(c)
    o_ref[...] = _dot3(s, w_ref[...]) + b_ref[...]


def _ada_call(cvec, ada_w, ada_b):
    depth, d, d6 = ada_w.shape
    tn = d6 // 4
    return pl.pallas_call(
        _ada_kernel,
        grid=(depth, d6 // tn),
        in_specs=[
            pl.BlockSpec((8, d), lambda l, j: (0, 0)),
            pl.BlockSpec((None, d, tn), lambda l, j: (l, 0, j)),
            pl.BlockSpec((None, 1, tn), lambda l, j: (l, 0, j)),
        ],
        out_specs=pl.BlockSpec((None, 8, tn), lambda l, j: (l, 0, j)),
        out_shape=jax.ShapeDtypeStruct((depth, 8, d6), F32),
        compiler_params=_cparams(2, 48),
        name="ada_mod",
    )(cvec, ada_w, ada_b.reshape(depth, 1, d6))


def _proj_kernel(dh, x_ref, mod_ref, g_ref, w_ref, cb_ref, u_ref, rx_ref, gl_ref):
    h = _norm_mod(x_ref[...], g_ref[...], mod_ref[0:1, :], mod_ref[1:2, :]).astype(BF16)

    def proj(j):
        return _dot(h, w_ref[:, j * dh:(j + 1) * dh])

    cb_ref[...] = proj(0).astype(BF16)
    u_ref[...] = (proj(1) * proj(2)).astype(BF16)
    rx_ref[...] = proj(3).astype(BF16)
    rg = proj(4)
    gelu = 0.5 * rg * (1.0 + jnp.tanh(math.sqrt(2.0 / math.pi) * (rg + 0.044715 * rg * rg * rg)))
    gl_ref[...] = gelu.astype(BF16)


def _proj_call(xres, mod_l, norm_g, w_in, *, layer, tm, n_tiles, first_block, mod_index):
    d = xres.shape[1]
    d_in = w_in.shape[2]
    dh = d_in // 5
    out = jax.ShapeDtypeStruct((n_tiles * tm, dh), BF16)
    ospec = pl.BlockSpec((tm, dh), lambda i: (i, 0))
    return pl.pallas_call(
        functools.partial(_proj_kernel, dh),
        grid=(n_tiles,),
        in_specs=[
            pl.BlockSpec((tm, d), lambda i: (first_block + i, 0)),
            pl.BlockSpec((None, 6, d), lambda i: (mod_index(i), 0, 0)),
            pl.BlockSpec((1, d), lambda i: (0, 0)),
            pl.BlockSpec((None, d, d_in), lambda i: (layer, 0, 0)),
        ],
        out_specs=[ospec, ospec, ospec, ospec],
        out_shape=[out, out, out, out],
        compiler_params=_cparams(1, 56),
        name="mix_proj",
    )(xres, mod_l, norm_g, w_in)


def _scan_kernel(n_lat, n_ctx, rxl_ref, rxc_ref, gl_ref, gc_ref, cw_ref, cbias_ref, wg_ref,
                 gb_ref, lam_ref, recl_ref, recc_ref, xpl_ref, xpc_ref, ab_ref, cin_ref):
    nseg_l = n_lat // SEG
    seg_c = n_ctx // CTX_NSEG
    pitch_c = seg_c + 8
    base_l = CTX_NSEG * pitch_c
    group = min(MAX_GROUP, nseg_l)
    n_groups = nseg_l // group
    gsegs = min(GATE_SEGS, nseg_l)

    def stage(src_ref, dst_ref, n):
        dst_ref[0:8, :] = jnp.zeros((8, LANES), F32)
        dst_ref[n + 8:n + 16, :] = jnp.zeros((8, LANES), F32)
        rows = min(512, n)

        def body(k, _):
            r0 = pl.multiple_of(k * rows, rows)
            dst_ref[pl.ds(r0 + 8, rows), :] = src_ref[pl.ds(r0, rows), :].astype(F32)
            return 0

        lax.fori_loop(0, n // rows, body, 0)

    stage(rxl_ref, xpl_ref, n_lat)
    stage(rxc_ref, xpc_ref, n_ctx)

    lam = lam_ref[...]
    softplus = jnp.maximum(-lam, 0.0) + jnp.log1p(jnp.exp(-jnp.abs(lam)))
    cvec2 = (-LRU_C * math.log2(math.e)) * softplus
    cw = cw_ref[...]
    cbias = cbias_ref[...]
    gb = gb_ref[...]

    def gates(xp_ref, t0, nsegs, seg_len, dst0, pitch):
        nrows = nsegs * seg_len
        win = xp_ref[pl.ds(t0, nrows + 16), :]
        xr = (cw[0:1] * win[6:6 + nrows] + cw[1:2] * win[7:7 + nrows]
              + cw[2:3] * win[8:8 + nrows] + cw[3:4] * win[9:9 + nrows] + cbias)
        z = _dot(xr.astype(BF16), wg_ref[...]) + gb
        for d in range(2):
            r = _sigmoid(z[:, 2 * d * LANES:(2 * d + 1) * LANES])
            gi = _sigmoid(z[:, (2 * d + 1) * LANES:(2 * d + 2) * LANES])
            a = jnp.exp2(cvec2[d:d + 1] * r)
            y = (1.0 - a) * (1.0 + a)
            b = jnp.where(y > 0.0, y * lax.rsqrt(y), 0.0) * (gi * xr)
            for s in range(nsegs):
                rows = pl.ds(dst0 + s * pitch, seg_len)
                ab_ref[2 * d, rows, :] = a[s * seg_len:(s + 1) * seg_len]
                ab_ref[2 * d + 1, rows, :] = b[s * seg_len:(s + 1) * seg_len]

    gates(xpc_ref, 0, CTX_NSEG, seg_c, 0, pitch_c)

    def gates_lat(j, _):
        gates(xpl_ref, pl.multiple_of(j * (gsegs * SEG), gsegs * SEG), gsegs, SEG,
              pl.multiple_of(base_l + j * (gsegs * SEG_PITCH), 8), SEG_PITCH)
        return 0

    lax.fori_loop(0, nseg_l // gsegs, gates_lat, 0)

    def scan_group(d, base, nseg, seg_len, pitch, seg0, carry, reverse):
        a_ref = ab_ref.at[2 * d]
        b_ref = ab_ref.at[2 * d + 1]

        def rows(k):
            s = (seg_len - 1 - k) if reverse else k
            return pl.ds(base + s, nseg, stride=pitch)

        def summarise(k, hc):
            h, acc = hc
            a = a_ref[rows(k), :]
            return a * h + b_ref[rows(k), :], a * acc

        h_end, a_end = lax.fori_loop(
            0, seg_len, summarise,
            (jnp.zeros((nseg, LANES), F32), jnp.ones((nseg, LANES), F32)), unroll=SCAN_UNROLL)
        order = range(nseg - 1, -1, -1) if reverse else range(nseg)
        for j in order:
            cin_ref[seg0 + j:seg0 + j + 1, :] = carry
            carry = h_end[j:j + 1] + a_end[j:j + 1] * carry

        def write(k, h):
            h = a_ref[rows(k), :] * h + b_ref[rows(k), :]
            b_ref[rows(k), :] = h
            return h

        lax.fori_loop(0, seg_len, write, cin_ref[seg0:seg0 + nseg, :], unroll=SCAN_UNROLL)
        return carry

    for d, reverse in ((0, False), (1, True)):
        carry = jnp.zeros((1, LANES), F32)
        carry = scan_group(d, 0, CTX_NSEG, seg_c, pitch_c, 0, carry, reverse)
        groups = range(n_groups - 1, -1, -1) if reverse else range(n_groups)
        for gidx in groups:
            carry = scan_group(d, base_l + gidx * group * SEG_PITCH, group, SEG, SEG_PITCH,
                               CTX_NSEG + gidx * group, carry, reverse)

    def emit(g_ref, out_ref, t0, nsegs, seg_len, src0, pitch):
        for s in range(nsegs):
            src = pl.ds(src0 + s * pitch, seg_len)
            rows = pl.ds(t0 + s * seg_len, seg_len)
            h = ab_ref[1, src, :] + ab_ref[3, src, :]
            out_ref[rows, :] = (h * g_ref[rows, :].astype(F32)).astype(BF16)

    emit(gc_ref, recc_ref, 0, CTX_NSEG, seg_c, 0, pitch_c)

    def emit_lat(j, _):
        emit(gl_ref, recl_ref, pl.multiple_of(j * (gsegs * SEG), gsegs * SEG), gsegs, SEG,
             pl.multiple_of(base_l + j * (gsegs * SEG_PITCH), 8), SEG_PITCH)
        return 0

    lax.fori_loop(0, nseg_l // gsegs, emit_lat, 0)


def _scan_call(rx_l, rx_c, g_l, g_c, conv_w, conv_b, wgate, bgate, lam, *, batch):
    n_lat = rx_l.shape[0] // batch
    n_ctx = rx_c.shape[0] // batch
    d_rec = rx_l.shape[1]
    ncb = d_rec // LANES
    seg_c = n_ctx // CTX_NSEG
    ab_rows = CTX_NSEG * (seg_c + 8) + (n_lat // SEG) * SEG_PITCH
    nseg_tot = CTX_NSEG + n_lat // SEG
    lat = pl.BlockSpec((n_lat, LANES), lambda b, c: (b, c))
    ctx = pl.BlockSpec((n_ctx, LANES), lambda b, c: (b, c))
    return pl.pallas_call(
        functools.partial(_scan_kernel, n_lat, n_ctx),
        grid=(batch, ncb),
        in_specs=[
            lat, ctx, lat, ctx,
            pl.BlockSpec((4, LANES), lambda b, c: (0, c)),
            pl.BlockSpec((1, LANES), lambda b, c: (0, c)),
            pl.BlockSpec((None, LANES, 4 * LANES), lambda b, c: (c, 0, 0)),
            pl.BlockSpec((None, 1, 4 * LANES), lambda b, c: (c, 0, 0)),
            pl.BlockSpec((None, 2, LANES), lambda b, c: (c, 0, 0)),
        ],
        out_specs=[lat, ctx],
        out_shape=[jax.ShapeDtypeStruct(rx_l.shape, BF16),
                   jax.ShapeDtypeStruct(rx_c.shape, BF16)],
        scratch_shapes=[
            pltpu.VMEM((n_lat + 16, LANES), F32),
            pltpu.VMEM((n_ctx + 16, LANES), F32),
            pltpu.VMEM((4, ab_rows, LANES), F32),
            pltpu.VMEM((nseg_tot, LANES), F32),
        ],
        compiler_params=_cparams(2, 56),
        name="rglru_scan",
    )(rx_l, rx_c, g_l, g_c, conv_w, conv_b, wgate, bgate, lam)


def _gate_params(wa, ba, wi, bi, lam):
    n_heads = wa.shape[1]
    hpb = LANES // REC_HEAD_DIM
    ncb = n_heads // hpb

    def blockdiag(w):
        w = w.reshape(ncb, hpb, REC_HEAD_DIM, REC_HEAD_DIM)
        rows = []
        for p in range(hpb):
            cols = [w[:, p] if q == p else jnp.zeros_like(w[:, p]) for q in range(hpb)]
            rows.append(jnp.concatenate(cols, axis=2))
        return jnp.concatenate(rows, axis=1)

    wgate = jnp.concatenate(
        [blockdiag(wa[0]), blockdiag(wi[0]), blockdiag(wa[1]), blockdiag(wi[1])], axis=2)

    def per_block(v):
        return v.reshape(ncb, 1, LANES)

    bgate = jnp.concatenate(
        [per_block(ba[0]), per_block(bi[0]), per_block(ba[1]), per_block(bi[1])], axis=2)
    lam_b = jnp.transpose(lam.reshape(2, ncb, LANES), (1, 0, 2))
    return wgate.astype(BF16), bgate, lam_b


def _shifted_rows(u, period):
    n = u.shape[0]
    col = lax.broadcasted_iota(jnp.int32, u.shape, 0) % period
    left = jnp.where(col == 0, 0.0, pltpu.roll(u, 1, 0))
    right = jnp.where(col == period - 1, 0.0, pltpu.roll(u, n - 1, 0))
    return left, right


def _out_kernel(latent, tiles_per_b, *refs):
    if latent:
        x_ref, cb_ref, u_ref, up_ref, un_ref, rec_ref, w_ref, cw_ref, mod_ref, o_ref = refs
    else:
        x_ref, cb_ref, u_ref, rec_ref, w_ref, cw_ref, mod_ref, o_ref = refs
    tm, dc = u_ref.shape
    cw = cw_ref[...]
    if latent:
        half = dc // 2
        uh = u_ref[:, 0:half].astype(F32)
        left, right = _shifted_rows(uh, GRID_W)
        conv_h = cw[0:1, 0:half] * left + cw[1:2, 0:half] * uh + cw[2:3, 0:half] * right
        uv = u_ref[:, half:dc].astype(F32)
        t = pl.program_id(0) % tiles_per_b
        prev = jnp.where(t > 0, up_ref[...].astype(F32), 0.0)
        nxt = jnp.where(t < tiles_per_b - 1, un_ref[...].astype(F32), 0.0)
        up = jnp.concatenate([prev, uv[0:tm - GRID_W]], axis=0)
        down = jnp.concatenate([uv[GRID_W:tm], nxt], axis=0)
        conv_v = cw[0:1, half:dc] * up + cw[1:2, half:dc] * uv + cw[2:3, half:dc] * down
        conv = jnp.concatenate([conv_h, conv_v], axis=1)
    else:
        u = u_ref[...].astype(F32)
        left, right = _shifted_rows(u, tm)
        conv = cw[0:1] * left + cw[1:2] * u + cw[2:3] * right
    conv_out = (cb_ref[...].astype(F32) * conv).astype(BF16)
    out = _dot(conv_out, w_ref[0:dc, :]) + _dot(rec_ref[...], w_ref[dc:, :])
    o_ref[...] = x_ref[...] + mod_ref[2:3, :] * out


def _out_call(xres, cb, u, rec, w_out, conv_w, mod_l, *, layer, latent, tm, n_tiles, first_block,
              tiles_per_b, mod_index):
    d = xres.shape[1]
    dc = cb.shape[1]
    half = dc // 2
    tile = pl.BlockSpec((tm, dc), lambda i: (i, 0))
    xspec = pl.BlockSpec((tm, d), lambda i: (first_block + i, 0))
    in_specs = [xspec, tile, tile]
    args = [xres, cb, u]
    if latent:
        hb = tm // GRID_W
        last = n_tiles * hb - 1
        in_specs += [
            pl.BlockSpec((GRID_W, half), lambda i: (jnp.maximum(i * hb - 1, 0), 1)),
            pl.BlockSpec((GRID_W, half), lambda i: (jnp.minimum((i + 1) * hb, last), 1)),
        ]
        args += [u, u]
    in_specs += [
        tile,
        pl.BlockSpec((None,) + w_out.shape[1:], lambda i: (layer, 0, 0)),
        pl.BlockSpec(conv_w.shape, lambda i: (0, 0)),
        pl.BlockSpec((None, 6, d), lambda i: (mod_index(i), 0, 0)),
    ]
    args += [rec, w_out, conv_w, mod_l]
    return pl.pallas_call(
        functools.partial(_out_kernel, latent, tiles_per_b),
        grid=(n_tiles,),
        in_specs=in_specs,
        out_specs=xspec,
        out_shape=jax.ShapeDtypeStruct(xres.shape, F32),
        input_output_aliases={0: 0},
        compiler_params=_cparams(1, 56),
        name="mix_out",
    )(*args)


def _ffn_kernel(fc, x_ref, mod_ref, g_ref, wg_ref, wu_ref, wd_ref, o_ref):
    x = x_ref[...]
    h = _norm_mod(x, g_ref[...], mod_ref[3:4, :], mod_ref[4:5, :]).astype(BF16)
    f = wg_ref.shape[1]
    acc = jnp.zeros(x.shape, F32)
    for f0 in range(0, f, fc):
        f1 = min(f0 + fc, f)
        gate = _dot(h, wg_ref[:, f0:f1])
        up = _dot(h, wu_ref[:, f0:f1])
        act = (gate * _sigmoid(gate) * up).astype(BF16)
        acc = acc + _dot(act, wd_ref[f0:f1, :])
    o_ref[...] = x + mod_ref[5:6, :] * acc


def _ffn_call(xres, mod_l, norm_g, wg, wu, wd, *, layer, tm, mod_index):
    rows, d = xres.shape
    f = wg.shape[2]
    xspec = pl.BlockSpec((tm, d), lambda i: (i, 0))
    single = pl.Buffered(1)
    return pl.pallas_call(
        functools.partial(_ffn_kernel, 512),
        grid=(rows // tm,),
        in_specs=[
            xspec,
            pl.BlockSpec((None, 6, d), lambda i: (mod_index(i), 0, 0)),
            pl.BlockSpec((1, d), lambda i: (0, 0)),
            pl.BlockSpec((None, d, f), lambda i: (layer, 0, 0), pipeline_mode=single),
            pl.BlockSpec((None, d, f), lambda i: (layer, 0, 0), pipeline_mode=single),
            pl.BlockSpec((None, f, d), lambda i: (layer, 0, 0), pipeline_mode=single),
        ],
        out_specs=xspec,
        out_shape=jax.ShapeDtypeStruct(xres.shape, F32),
        input_output_aliases={0: 0},
        compiler_params=_cparams(1, 56),
        name="dense_ffn",
    )(xres, mod_l, norm_g, wg, wu, wd)


def _route_kernel(n_experts, x_ref, mod_ref, g_ref, rw_ref, rb_ref,
                  hb_ref, info_ref, infot_ref, cnt_ref):
    tb = x_ref.shape[0]
    h = _norm_mod(x_ref[...], g_ref[...], mod_ref[3:4, :], mod_ref[4:5, :])
    hb_ref[...] = h.astype(BF16)
    logits = _dot3(h, rw_ref[...]) + rb_ref[...]
    lane = lax.broadcasted_iota(jnp.int32, logits.shape, 1)
    logits = jnp.where(lane < n_experts, logits, -jnp.inf)
    m1 = jnp.max(logits, axis=1, keepdims=True)
    i1 = jnp.min(jnp.where(logits == m1, lane, ROUTER_LANES), axis=1, keepdims=True)
    rest = jnp.where(lane == i1, -jnp.inf, logits)
    m2 = jnp.max(rest, axis=1, keepdims=True)
    i2 = jnp.min(jnp.where(rest == m2, lane, ROUTER_LANES), axis=1, keepdims=True)
    e2 = jnp.exp(m2 - m1)
    p1 = 1.0 / (1.0 + e2)
    p2 = e2 * p1

    member = jnp.where((lane == i1) | (lane == i2), 1.0, 0.0)
    earlier = jnp.where(lax.broadcasted_iota(jnp.int32, (tb, tb), 0)
                        > lax.broadcasted_iota(jnp.int32, (tb, tb), 1), 1.0, 0.0)
    rank = _dot(earlier.astype(BF16), member.astype(BF16))
    cnt = jnp.sum(member, axis=0, keepdims=True)
    units = jnp.floor((cnt + (SEG_ALIGN - 1.0)) * (1.0 / SEG_ALIGN))
    before = jnp.where(lax.broadcasted_iota(jnp.int32, (ROUTER_LANES, ROUTER_LANES), 0)
                       < lax.broadcasted_iota(jnp.int32, (ROUTER_LANES, ROUTER_LANES), 1), 1.0, 0.0)
    start = _dot(jnp.broadcast_to(units, (8, ROUTER_LANES)).astype(BF16),
                 before.astype(BF16))[0:1, :] * float(SEG_ALIGN)
    pos = start + rank
    dest1 = jnp.sum(jnp.where(lane == i1, pos, 0.0), axis=1, keepdims=True)
    dest2 = jnp.sum(jnp.where(lane == i2, pos, 0.0), axis=1, keepdims=True)
    info = (jnp.where(lane == 0, dest1, 0.0) + jnp.where(lane == 1, dest2, 0.0)
            + jnp.where(lane == 2, p1, 0.0) + jnp.where(lane == 3, p2, 0.0))
    info_ref[...] = info
    infot_ref[...] = info.T[0:8, :]
    cnt_ref[...] = jnp.broadcast_to(cnt, (8, ROUTER_LANES))


def _route_call(xres, mod_l, norm_g, rw, rb, *, n_blocks, n_experts, mod_index):
    d = xres.shape[1]
    tb = ROUTE_BLOCK
    rows = n_blocks * tb
    return pl.pallas_call(
        functools.partial(_route_kernel, n_experts),
        grid=(n_blocks,),
        in_specs=[
            pl.BlockSpec((tb, d), lambda i: (i, 0)),
            pl.BlockSpec((None, 6, d), lambda i: (mod_index(i), 0, 0)),
            pl.BlockSpec((1, d), lambda i: (0, 0)),
            pl.BlockSpec((d, ROUTER_LANES), lambda i: (0, 0)),
            pl.BlockSpec((1, ROUTER_LANES), lambda i: (0, 0)),
        ],
        out_specs=[
            pl.BlockSpec((tb, d), lambda i: (i, 0)),
            pl.BlockSpec((tb, ROUTER_LANES), lambda i: (i, 0)),
            pl.BlockSpec((None, 8, tb), lambda i: (i, 0, 0)),
            pl.BlockSpec((None, 8, ROUTER_LANES), lambda i: (i, 0, 0)),
        ],
        out_shape=[
            jax.ShapeDtypeStruct((rows, d), BF16),
            jax.ShapeDtypeStruct((rows, ROUTER_LANES), F32),
            jax.ShapeDtypeStruct((n_blocks, 8, tb), F32),
            jax.ShapeDtypeStruct((n_blocks, 8, ROUTER_LANES), F32),
        ],
        compiler_params=_cparams(1, 48),
        name="moe_route",
    )(xres, mod_l, norm_g, rw, rb)


def _segment_copies(n_experts, b, cntp_sm, loff_sm, dst_sm, local_ref, global_ref, sem, to_global):
    copies = []
    for e in range(n_experts):
        cp = cntp_sm[b * n_experts + e]
        lo = loff_sm[b * n_experts + e]
        go = dst_sm[b * n_experts + e]
        pos = jnp.int32(0)
        size = ROUTE_BLOCK
        while size >= SEG_ALIGN:
            bit = (cp & size) != 0
            loc = local_ref.at[pl.ds(pl.multiple_of(lo + pos, SEG_ALIGN), size)]
            glo = global_ref.at[pl.ds(pl.multiple_of(go + pos, SEG_ALIGN), size)]
            cpy = (pltpu.make_async_copy(loc, glo, sem) if to_global
                   else pltpu.make_async_copy(glo, loc, sem))
            copies.append((bit, cpy))
            pos = pos + jnp.where(bit, size, 0)
            size //= 2
    return copies


def _sort_kernel(n_experts, cntp_sm, loff_sm, dst_sm, hb_ref, infot_ref, zero_ref, sorted_ref,
                 tile_ref, sem):
    del zero_ref
    b = pl.program_id(0)
    s_loc, tb = tile_ref.shape[0], hb_ref.shape[0]
    d1 = infot_ref[0:1, :].astype(jnp.int32)
    d2 = infot_ref[1:2, :].astype(jnp.int32)
    r = lax.broadcasted_iota(jnp.int32, (s_loc, tb), 0)
    perm = jnp.where((r == d1) | (r == d2), 1.0, 0.0).astype(BF16)
    tile_ref[...] = _dot(perm, hb_ref[...]).astype(BF16)
    copies = _segment_copies(n_experts, b, cntp_sm, loff_sm, dst_sm, tile_ref, sorted_ref, sem, True)
    for bit, cpy in copies:
        pl.when(bit)(cpy.start)
    for bit, cpy in copies:
        pl.when(bit)(cpy.wait)


def _sort_call(cntp, loff, dst, hb, infot, sorted_init, *, n_blocks, n_experts, s_loc):
    d = hb.shape[1]
    tb = ROUTE_BLOCK
    return pl.pallas_call(
        functools.partial(_sort_kernel, n_experts),
        grid_spec=pltpu.PrefetchScalarGridSpec(
            num_scalar_prefetch=3,
            grid=(n_blocks,),
            in_specs=[
                pl.BlockSpec((tb, d), lambda i, *_: (i, 0)),
                pl.BlockSpec((None, 8, tb), lambda i, *_: (i, 0, 0)),
                pl.BlockSpec(memory_space=pl.ANY),
            ],
            out_specs=pl.BlockSpec(memory_space=pl.ANY),
            scratch_shapes=[pltpu.VMEM((s_loc, d), BF16), pltpu.SemaphoreType.DMA(())],
        ),
        out_shape=jax.ShapeDtypeStruct(sorted_init.shape, BF16),
        input_output_aliases={5: 0},
        compiler_params=_cparams(1, 48),
        name="moe_sort",
    )(cntp, loff, dst, hb, infot, sorted_init)


def _expert_kernel(te_sm, valid_sm, xs_ref, wg_ref, wu_ref, wd_ref, o_ref, acc_ref):
    del te_sm
    i = pl.program_id(0)
    k = pl.program_id(1)
    last = pl.num_programs(1) - 1
    valid = valid_sm[i] != 0

    @pl.when(valid)
    def _():
        x = xs_ref[...]
        gate = _dot(x, wg_ref[...].astype(BF16))
        up = _dot(x, wu_ref[...].astype(BF16))
        act = (gate * _sigmoid(gate) * up).astype(BF16)
        y = _dot(act, wd_ref[...].astype(BF16))

        @pl.when(k == 0)
        def _():
            acc_ref[...] = y

        @pl.when(k > 0)
        def _():
            acc_ref[...] += y

        @pl.when(k == last)
        def _():
            o_ref[...] = acc_ref[...].astype(BF16)

    @pl.when(jnp.logical_not(valid) & (k == last))
    def _():
        o_ref[...] = jnp.zeros(o_ref.shape, BF16)


def _expert_call(tile_expert, tile_valid, xs, wg, wu, wd, *, layer, tmx, fc):
    rows, d = xs.shape
    n_k = wg.shape[3] // fc

    def kk(i, k, te, tv):
        return jnp.where(tv[i] != 0, k, n_k - 1)

    up_spec = pl.BlockSpec((None, None, d, fc),
                           lambda i, k, te, tv: (layer, te[i], 0, kk(i, k, te, tv)))
    return pl.pallas_call(
        _expert_kernel,
        grid_spec=pltpu.PrefetchScalarGridSpec(
            num_scalar_prefetch=2,
            grid=(rows // tmx, n_k),
            in_specs=[
                pl.BlockSpec((tmx, d), lambda i, k, te, tv: (i, 0)),
                up_spec,
                up_spec,
                pl.BlockSpec((None, None, fc, d),
                             lambda i, k, te, tv: (layer, te[i], kk(i, k, te, tv), 0)),
            ],
            out_specs=pl.BlockSpec((tmx, d), lambda i, k, te, tv: (i, 0)),
            scratch_shapes=[pltpu.VMEM((tmx, d), F32)],
        ),
        out_shape=jax.ShapeDtypeStruct((rows, d), BF16),
        compiler_params=_cparams(2, 56),
        name="moe_experts",
    )(tile_expert, tile_valid, xs, wg, wu, wd)


def _combine_kernel(n_experts, final, cntp_sm, loff_sm, dst_sm, x_ref, info_ref, mod_ref, *rest):
    if final:
        fg_ref, ys_ref, o_ref, tile_ref, sem = rest
    else:
        ys_ref, o_ref, tile_ref, sem = rest
    b = pl.program_id(0)
    s_loc, tb = tile_ref.shape[0], x_ref.shape[0]
    tile_ref[...] = jnp.zeros(tile_ref.shape, BF16)
    copies = _segment_copies(n_experts, b, cntp_sm, loff_sm, dst_sm, tile_ref, ys_ref, sem, False)
    for bit, cpy in copies:
        pl.when(bit)(cpy.start)
    info = info_ref[...]
    d1 = info[:, 0:1].astype(jnp.int32)
    d2 = info[:, 1:2].astype(jnp.int32)
    c = lax.broadcasted_iota(jnp.int32, (tb, s_loc), 1)
    pg = (jnp.where(c == d1, info[:, 2:3], 0.0) + jnp.where(c == d2, info[:, 3:4], 0.0)).astype(BF16)
    for bit, cpy in copies:
        pl.when(bit)(cpy.wait)
    y = _dot(pg, tile_ref[...])
    xn = x_ref[...] + mod_ref[5:6, :] * y
    if final:
        ms = jnp.mean(xn * xn, axis=-1, keepdims=True)
        xn = xn * lax.rsqrt(ms + EPS) * fg_ref[...]
    o_ref[...] = xn


def _combine_call(cntp, loff, dst, xres, info, mod_l, ys, final_g, *, n_blocks, n_experts,
                  s_loc, mod_index):
    d = xres.shape[1]
    tb = ROUTE_BLOCK
    final = final_g is not None
    xspec = pl.BlockSpec((tb, d), lambda i, *_: (i, 0))
    in_specs = [
        xspec,
        pl.BlockSpec((tb, ROUTER_LANES), lambda i, *_: (i, 0)),
        pl.BlockSpec((None, 6, d), lambda i, *_: (mod_index(i), 0, 0)),
    ]
    args = [xres, info, mod_l]
    if final:
        in_specs.append(pl.BlockSpec((1, d), lambda i, *_: (0, 0)))
        args.append(final_g)
    in_specs.append(pl.BlockSpec(memory_space=pl.ANY))
    args.append(ys)
    out_rows = n_blocks * tb if final else xres.shape[0]
    return pl.pallas_call(
        functools.partial(_combine_kernel, n_experts, final),
        grid_spec=pltpu.PrefetchScalarGridSpec(
            num_scalar_prefetch=3,
            grid=(n_blocks,),
            in_specs=in_specs,
            out_specs=xspec,
            scratch_shapes=[pltpu.VMEM((s_loc, d), BF16), pltpu.SemaphoreType.DMA(())],
        ),
        out_shape=jax.ShapeDtypeStruct((out_rows, d), F32),
        input_output_aliases={} if final else {3: 0},
        compiler_params=_cparams(1, 48),
        name="moe_combine",
    )(cntp, loff, dst, *args)


def _moe_layout(cnt, n_tiles, tmx):
    cntp = (cnt + (SEG_ALIGN - 1)) // SEG_ALIGN * SEG_ALIGN
    loff = jnp.cumsum(cntp, axis=1) - cntp
    total = jnp.sum(cntp, axis=0)
    region = (total + (tmx - 1)) // tmx * tmx
    region_end = jnp.cumsum(region)
    dst = (region_end - region)[None, :] + jnp.cumsum(cntp, axis=0) - cntp
    tile_start = jnp.arange(n_tiles, dtype=jnp.int32) * tmx
    valid = tile_start < region_end[-1]
    last_tile = jnp.maximum(region_end[-1] // tmx - 1, 0) * tmx
    start_eff = jnp.where(valid, tile_start, last_tile)
    owner = jnp.sum((start_eff[:, None] >= region_end[None, :]).astype(jnp.int32), axis=1)
    flat = lambda v: v.reshape(-1).astype(jnp.int32)
    return flat(cntp), flat(loff), flat(dst), owner.astype(jnp.int32), valid.astype(jnp.int32)


def _moe_call(xres, mod_l, norm_g, rw, rb, wg, wu, wd, final_g, *, layer, n_blocks, mod_index):
    d = xres.shape[1]
    n_experts = wg.shape[1]
    tmx = EXPERT_ROWS
    block_rows = 2 * ROUTE_BLOCK + n_experts * (SEG_ALIGN - 1)
    s_loc = -(-block_rows // LANES) * LANES
    max_rows = n_blocks * block_rows + n_experts * (tmx - SEG_ALIGN)
    n_tiles = max_rows // tmx
    hb, info, infot, cnt = _route_call(xres, mod_l, norm_g, rw, rb, n_blocks=n_blocks,
                                       n_experts=n_experts, mod_index=mod_index)
    cnt = cnt[:, 0, :n_experts].astype(jnp.int32)
    cntp, loff, dst, owner, valid = _moe_layout(cnt, n_tiles, tmx)
    xs = _sort_call(cntp, loff, dst, hb, infot, jnp.zeros((n_tiles * tmx, d), BF16),
                    n_blocks=n_blocks, n_experts=n_experts, s_loc=s_loc)
    ys = _expert_call(owner, valid, xs, wg, wu, wd, layer=layer, tmx=tmx, fc=EXPERT_FC)
    return _combine_call(cntp, loff, dst, xres, info, mod_l, ys, final_g, n_blocks=n_blocks,
                         n_experts=n_experts, s_loc=s_loc, mod_index=mod_index)


def _final_kernel(x_ref, g_ref, o_ref):
    x = x_ref[...]
    ms = jnp.mean(x * x, axis=-1, keepdims=True)
    o_ref[...] = x * lax.rsqrt(ms + EPS) * g_ref[...]


def _final_call(xres, g, *, rows, tm):
    d = xres.shape[1]
    spec = pl.BlockSpec((tm, d), lambda i: (i, 0))
    return pl.pallas_call(
        _final_kernel,
        grid=(rows // tm,),
        in_specs=[spec, pl.BlockSpec((1, d), lambda i: (0, 0))],
        out_specs=spec,
        out_shape=jax.ShapeDtypeStruct((rows, d), F32),
        compiler_params=_cparams(1, 48),
        name="final_norm",
    )(xres, g)


def kernel(x, c, ctx, c_ctx, ada_w, ada_b, norm_mix_g, norm_ffn_g, w_in, w_out, conv_w,
           rec_conv_w, rec_conv_b, lru_wa, lru_ba, lru_wi, lru_bi, lru_lam,
           dense_wg, dense_wu, dense_wd, router_w, router_b, moe_wg, moe_wu, moe_wd, final_g):
    batch, n_lat, d = x.shape
    n_ctx = ctx.shape[1]
    depth = ada_w.shape[0]
    n_experts = router_w.shape[2]
    rows_lat = batch * n_lat
    rows_ctx = batch * n_ctx
    tm = min(512, n_lat)
    tiles_per_b = n_lat // tm
    ctx_block0 = rows_lat // n_ctx
    tb = ROUTE_BLOCK
    assert n_lat % tb == 0 and rows_ctx % tb == 0 and batch < 8

    xres = jnp.concatenate([x.reshape(rows_lat, d), ctx.reshape(rows_ctx, d)], axis=0)
    cvec = jnp.zeros((8, d), F32).at[:batch].set(c).at[batch].set(c_ctx)
    mod = _ada_call(cvec, ada_w, ada_b).reshape(depth, 8, 6, d)

    def lat_index(i):
        return i // tiles_per_b

    def ctx_index(i):
        return batch

    def all_index(i):
        return jnp.where(i < rows_lat // tb, (i * tb) // n_lat, batch)

    rw = jnp.pad(router_w, ((0, 0), (0, 0), (0, ROUTER_LANES - n_experts)))
    rb = jnp.pad(router_b, ((0, 0), (0, ROUTER_LANES - n_experts)))[:, None, :]

    w_in_b = w_in.astype(BF16)
    w_out_b = w_out.astype(BF16)
    dense_b = [w.astype(BF16) for w in (dense_wg, dense_wu, dense_wd)]

    out = None
    for l in range(depth):
        last = l == depth - 1
        mod_l = mod[l]
        gmix = norm_mix_g[l][None, :]
        gffn = norm_ffn_g[l][None, :]

        cb_l, u_l, rx_l, g_l = _proj_call(xres, mod_l, gmix, w_in_b, layer=l, tm=tm,
                                          n_tiles=rows_lat // tm, first_block=0,
                                          mod_index=lat_index)
        cb_c, u_c, rx_c, g_c = _proj_call(xres, mod_l, gmix, w_in_b, layer=l, tm=n_ctx,
                                          n_tiles=batch, first_block=ctx_block0,
                                          mod_index=ctx_index)
        wgate, bgate, lam_b = _gate_params(lru_wa[l], lru_ba[l], lru_wi[l], lru_bi[l], lru_lam[l])
        rec_l, rec_c = _scan_call(rx_l, rx_c, g_l, g_c, rec_conv_w[l], rec_conv_b[l][None, :],
                                  wgate, bgate, lam_b, batch=batch)
        xres = _out_call(xres, cb_l, u_l, rec_l, w_out_b, conv_w[l], mod_l, layer=l, latent=True,
                         tm=tm, n_tiles=rows_lat // tm, first_block=0, tiles_per_b=tiles_per_b,
                         mod_index=lat_index)
        if not last:
            xres = _out_call(xres, cb_c, u_c, rec_c, w_out_b, conv_w[l], mod_l, layer=l,
                             latent=False, tm=n_ctx, n_tiles=batch, first_block=ctx_block0,
                             tiles_per_b=1, mod_index=ctx_index)

        j = l // 2
        if l % 2 == 0:
            xres = _ffn_call(xres, mod_l, gffn, *dense_b, layer=j, tm=tb, mod_index=all_index)
        else:
            n_blocks = rows_lat // tb if last else (rows_lat + rows_ctx) // tb
            res = _moe_call(xres, mod_l, gffn, rw[j], rb[j], moe_wg, moe_wu, moe_wd,
                            final_g[None, :] if last else None, layer=j, n_blocks=n_blocks,
                            mod_index=all_index)
            if last:
                out = res
            else:
                xres = res

    if out is None:
        out = _final_call(xres, final_g[None, :], rows=rows_lat, tm=tm)
    return out.reshape(batch, n_lat, d)
```

```python
import functools
import math

import jax
import jax.numpy as jnp
from jax import lax
from jax.experimental import pallas as pl
from jax.experimental.pallas import tpu as pltpu

GRID_W = 64
LRU_C = 8.0
EPS = 1e-6
REC_HEAD_DIM = 64

LANES = 128
SEG = 128
SEG_PITCH = SEG + 8
CTX_NSEG = 8
MAX_GROUP = 64
GATE_SEGS = 4
SCAN_UNROLL = 8
ROUTER_LANES = 128
ROUTE_BLOCK = 512
SEG_ALIGN = 16
EXPERT_ROWS = 512
FFN_CHUNK = 512

F32 = jnp.float32
BF16 = jnp.bfloat16


def _cparams(n_grid, vmem_mb):
    return pltpu.CompilerParams(
        dimension_semantics=("arbitrary",) * n_grid,
        vmem_limit_bytes=vmem_mb * 1024 * 1024)


def _dot(a, b):
    return jnp.dot(a, b, preferred_element_type=F32)


def _sigmoid(z):
    return 1.0 / (1.0 + jnp.exp2(z * (-math.log2(math.e))))


def _split_bf16(v):
    hi = v.astype(BF16)
    lo = (v - hi.astype(F32)).astype(BF16)
    return hi, lo


def _dot3(a, b):
    a_hi, a_lo = _split_bf16(a)
    b_hi, b_lo = _split_bf16(b)
    return _dot(a_hi, b_hi) + _dot(a_hi, b_lo) + _dot(a_lo, b_hi)


def _norm_mod(x, g, shift, scale):
    ms = jnp.mean(x * x, axis=-1, keepdims=True)
    y = x * lax.rsqrt(ms + EPS) * g
    return y * (1.0 + scale) + shift


def _ada_kernel(c_ref, w_ref, b_ref, o_ref):
    c = c_ref[...]
    s = c * _sigmoid(c)
    o_ref[...] = _dot3(s, w_ref[...]) + b_ref[...]


def _ada_call(cvec, ada_w, ada_b):
    depth, d, d6 = ada_w.shape
    tn = d6 // 4
    return pl.pallas_call(
        _ada_kernel,
        grid=(depth, d6 // tn),
        in_specs=[
            pl.BlockSpec((8, d), lambda l, j: (0, 0)),
            pl.BlockSpec((None, d, tn), lambda l, j: (l, 0, j)),
            pl.BlockSpec((None, 1, tn), lambda l, j: (l, 0, j)),
        ],
        out_specs=pl.BlockSpec((None, 8, tn), lambda l, j: (l, 0, j)),
        out_shape=jax.ShapeDtypeStruct((depth, 8, d6), F32),
        compiler_params=_cparams(2, 48),
        name="ada_mod",
    )(cvec, ada_w, ada_b.reshape(depth, 1, d6))


def _proj_kernel(dh, x_ref, mod_ref, g_ref, w_ref, cb_ref, u_ref, rx_ref, gl_ref):
    h = _norm_mod(x_ref[...], g_ref[...], mod_ref[0:1, :], mod_ref[1:2, :]).astype(BF16)

    def proj(j):
        return _dot(h, w_ref[:, j * dh:(j + 1) * dh])

    cb_ref[...] = proj(0).astype(BF16)
    u_ref[...] = (proj(1) * proj(2)).astype(BF16)
    rx_ref[...] = proj(3).astype(BF16)
    rg = proj(4)
    gelu = 0.5 * rg * (1.0 + jnp.tanh(math.sqrt(2.0 / math.pi) * (rg + 0.044715 * rg * rg * rg)))
    gl_ref[...] = gelu.astype(BF16)


def _proj_call(xres, mod_l, norm_g, w_in, *, layer, tm, n_tiles, first_block, mod_index):
    d = xres.shape[1]
    d_in = w_in.shape[2]
    dh = d_in // 5
    out = jax.ShapeDtypeStruct((n_tiles * tm, dh), BF16)
    ospec = pl.BlockSpec((tm, dh), lambda i: (i, 0))
    return pl.pallas_call(
        functools.partial(_proj_kernel, dh),
        grid=(n_tiles,),
        in_specs=[
            pl.BlockSpec((tm, d), lambda i: (first_block + i, 0)),
            pl.BlockSpec((None, 6, d), lambda i: (mod_index(i), 0, 0)),
            pl.BlockSpec((1, d), lambda i: (0, 0)),
            pl.BlockSpec((None, d, d_in), lambda i: (layer, 0, 0)),
        ],
        out_specs=[ospec, ospec, ospec, ospec],
        out_shape=[out, out, out, out],
        compiler_params=_cparams(1, 56),
        name="mix_proj",
    )(xres, mod_l, norm_g, w_in)


def _scan_kernel(n_lat, n_ctx, rxl_ref, rxc_ref, gl_ref, gc_ref, cw_ref, cbias_ref, wg_ref,
                 gb_ref, lam_ref, recl_ref, recc_ref, xpl_ref, xpc_ref, ab_ref, cin_ref):
    nseg_l = n_lat // SEG
    seg_c = n_ctx // CTX_NSEG
    pitch_c = seg_c + 8
    base_l = CTX_NSEG * pitch_c
    group = min(MAX_GROUP, nseg_l)
    n_groups = nseg_l // group
    gsegs = min(GATE_SEGS, nseg_l)

    def stage(src_ref, dst_ref, n):
        dst_ref[0:8, :] = jnp.zeros((8, LANES), F32)
        dst_ref[n + 8:n + 16, :] = jnp.zeros((8, LANES), F32)
        rows = min(512, n)

        def body(k, _):
            r0 = pl.multiple_of(k * rows, rows)
            dst_ref[pl.ds(r0 + 8, rows), :] = src_ref[pl.ds(r0, rows), :].astype(F32)
            return 0

        lax.fori_loop(0, n // rows, body, 0)

    stage(rxl_ref, xpl_ref, n_lat)
    stage(rxc_ref, xpc_ref, n_ctx)

    lam = lam_ref[...]
    softplus = jnp.maximum(-lam, 0.0) + jnp.log1p(jnp.exp(-jnp.abs(lam)))
    cvec2 = (-LRU_C * math.log2(math.e)) * softplus
    cw = cw_ref[...]
    cbias = cbias_ref[...]
    gb = gb_ref[...]

    def gates(xp_ref, t0, nsegs, seg_len, dst0, pitch):
        nrows = nsegs * seg_len
        win = xp_ref[pl.ds(t0, nrows + 16), :]
        xr = (cw[0:1] * win[6:6 + nrows] + cw[1:2] * win[7:7 + nrows]
              + cw[2:3] * win[8:8 + nrows] + cw[3:4] * win[9:9 + nrows] + cbias)
        z = _dot(xr.astype(BF16), wg_ref[...]) + gb
        for d in range(2):
            r = _sigmoid(z[:, 2 * d * LANES:(2 * d + 1) * LANES])
            gi = _sigmoid(z[:, (2 * d + 1) * LANES:(2 * d + 2) * LANES])
            a = jnp.exp2(cvec2[d:d + 1] * r)
            y = (1.0 - a) * (1.0 + a)
            b = jnp.where(y > 0.0, y * lax.rsqrt(y), 0.0) * (gi * xr)
            for s in range(nsegs):
                rows = pl.ds(dst0 + s * pitch, seg_len)
                ab_ref[2 * d, rows, :] = a[s * seg_len:(s + 1) * seg_len]
                ab_ref[2 * d + 1, rows, :] = b[s * seg_len:(s + 1) * seg_len]

    gates(xpc_ref, 0, CTX_NSEG, seg_c, 0, pitch_c)

    def gates_lat(j, _):
        gates(xpl_ref, pl.multiple_of(j * (gsegs * SEG), gsegs * SEG), gsegs, SEG,
              pl.multiple_of(base_l + j * (gsegs * SEG_PITCH), 8), SEG_PITCH)
        return 0

    lax.fori_loop(0, nseg_l // gsegs, gates_lat, 0)

    def scan_group(d, base, nseg, seg_len, pitch, seg0, carry, reverse):
        a_ref = ab_ref.at[2 * d]
        b_ref = ab_ref.at[2 * d + 1]

        def rows(k):
            s = (seg_len - 1 - k) if reverse else k
            return pl.ds(base + s, nseg, stride=pitch)

        def summarise(k, hc):
            h, acc = hc
            a = a_ref[rows(k), :]
            return a * h + b_ref[rows(k), :], a * acc

        h_end, a_end = lax.fori_loop(
            0, seg_len, summarise,
            (jnp.zeros((nseg, LANES), F32), jnp.ones((nseg, LANES), F32)), unroll=SCAN_UNROLL)
        order = range(nseg - 1, -1, -1) if reverse else range(nseg)
        for j in order:
            cin_ref[seg0 + j:seg0 + j + 1, :] = carry
            carry = h_end[j:j + 1] + a_end[j:j + 1] * carry

        def write(k, h):
            h = a_ref[rows(k), :] * h + b_ref[rows(k), :]
            b_ref[rows(k), :] = h
            return h

        lax.fori_loop(0, seg_len, write, cin_ref[seg0:seg0 + nseg, :], unroll=SCAN_UNROLL)
        return carry

    for d, reverse in ((0, False), (1, True)):
        carry = jnp.zeros((1, LANES), F32)
        carry = scan_group(d, 0, CTX_NSEG, seg_c, pitch_c, 0, carry, reverse)
        groups = range(n_groups - 1, -1, -1) if reverse else range(n_groups)
        for gidx in groups:
            carry = scan_group(d, base_l + gidx * group * SEG_PITCH, group, SEG, SEG_PITCH,
                               CTX_NSEG + gidx * group, carry, reverse)

    def emit(g_ref, out_ref, t0, nsegs, seg_len, src0, pitch):
        for s in range(nsegs):
            src = pl.ds(src0 + s * pitch, seg_len)
            rows = pl.ds(t0 + s * seg_len, seg_len)
            h = ab_ref[1, src, :] + ab_ref[3, src, :]
            out_ref[rows, :] = (h * g_ref[rows, :].astype(F32)).astype(BF16)

    emit(gc_ref, recc_ref, 0, CTX_NSEG, seg_c, 0, pitch_c)

    def emit_lat(j, _):
        emit(gl_ref, recl_ref, pl.multiple_of(j * (gsegs * SEG), gsegs * SEG), gsegs, SEG,
             pl.multiple_of(base_l + j * (gsegs * SEG_PITCH), 8), SEG_PITCH)
        return 0

    lax.fori_loop(0, nseg_l // gsegs, emit_lat, 0)


def _scan_call(rx_l, rx_c, g_l, g_c, conv_w, conv_b, wgate, bgate, lam, *, batch):
    n_lat = rx_l.shape[0] // batch
    n_ctx = rx_c.shape[0] // batch
    d_rec = rx_l.shape[1]
    ncb = d_rec // LANES
    seg_c = n_ctx // CTX_NSEG
    ab_rows = CTX_NSEG * (seg_c + 8) + (n_lat // SEG) * SEG_PITCH
    nseg_tot = CTX_NSEG + n_lat // SEG
    lat = pl.BlockSpec((n_lat, LANES), lambda b, c: (b, c))
    ctx = pl.BlockSpec((n_ctx, LANES), lambda b, c: (b, c))
    return pl.pallas_call(
        functools.partial(_scan_kernel, n_lat, n_ctx),
        grid=(batch, ncb),
        in_specs=[
            lat, ctx, lat, ctx,
            pl.BlockSpec((4, LANES), lambda b, c: (0, c)),
            pl.BlockSpec((1, LANES), lambda b, c: (0, c)),
            pl.BlockSpec((None, LANES, 4 * LANES), lambda b, c: (c, 0, 0)),
            pl.BlockSpec((None, 1, 4 * LANES), lambda b, c: (c, 0, 0)),
            pl.BlockSpec((None, 2, LANES), lambda b, c: (c, 0, 0)),
        ],
        out_specs=[lat, ctx],
        out_shape=[jax.ShapeDtypeStruct(rx_l.shape, BF16),
                   jax.ShapeDtypeStruct(rx_c.shape, BF16)],
        scratch_shapes=[
            pltpu.VMEM((n_lat + 16, LANES), F32),
            pltpu.VMEM((n_ctx + 16, LANES), F32),
            pltpu.VMEM((4, ab_rows, LANES), F32),
            pltpu.VMEM((nseg_tot, LANES), F32),
        ],
        compiler_params=_cparams(2, 56),
        name="rglru_scan",
    )(rx_l, rx_c, g_l, g_c, conv_w, conv_b, wgate, bgate, lam)


def _gate_params(wa, ba, wi, bi, lam):
    n_heads = wa.shape[1]
    hpb = LANES // REC_HEAD_DIM
    ncb = n_heads // hpb

    def blockdiag(w):
        w = w.reshape(ncb, hpb, REC_HEAD_DIM, REC_HEAD_DIM)
        rows = []
        for p in range(hpb):
            cols = [w[:, p] if q == p else jnp.zeros_like(w[:, p]) for q in range(hpb)]
            rows.append(jnp.concatenate(cols, axis=2))
        return jnp.concatenate(rows, axis=1)

    wgate = jnp.concatenate(
        [blockdiag(wa[0]), blockdiag(wi[0]), blockdiag(wa[1]), blockdiag(wi[1])], axis=2)

    def per_block(v):
        return v.reshape(ncb, 1, LANES)

    bgate = jnp.concatenate(
        [per_block(ba[0]), per_block(bi[0]), per_block(ba[1]), per_block(bi[1])], axis=2)
    lam_b = jnp.transpose(lam.reshape(2, ncb, LANES), (1, 0, 2))
    return wgate.astype(BF16), bgate, lam_b


def _shifted_rows(u, period):
    n = u.shape[0]
    col = lax.broadcasted_iota(jnp.int32, u.shape, 0) % period
    left = jnp.where(col == 0, 0.0, pltpu.roll(u, 1, 0))
    right = jnp.where(col == period - 1, 0.0, pltpu.roll(u, n - 1, 0))
    return left, right


def _out_kernel(latent, tiles_per_b, *refs):
    if latent:
        x_ref, cb_ref, u_ref, up_ref, un_ref, rec_ref, w_ref, cw_ref, mod_ref, o_ref = refs
    else:
        x_ref, cb_ref, u_ref, rec_ref, w_ref, cw_ref, mod_ref, o_ref = refs
    tm, dc = u_ref.shape
    cw = cw_ref[...]
    if latent:
        half = dc // 2
        uh = u_ref[:, 0:half].astype(F32)
        left, right = _shifted_rows(uh, GRID_W)
        conv_h = cw[0:1, 0:half] * left + cw[1:2, 0:half] * uh + cw[2:3, 0:half] * right
        uv = u_ref[:, half:dc].astype(F32)
        t = pl.program_id(0) % tiles_per_b
        prev = jnp.where(t > 0, up_ref[...].astype(F32), 0.0)
        nxt = jnp.where(t < tiles_per_b - 1, un_ref[...].astype(F32), 0.0)
        up = jnp.concatenate([prev, uv[0:tm - GRID_W]], axis=0)
        down = jnp.concatenate([uv[GRID_W:tm], nxt], axis=0)
        conv_v = cw[0:1, half:dc] * up + cw[1:2, half:dc] * uv + cw[2:3, half:dc] * down
        conv = jnp.concatenate([conv_h, conv_v], axis=1)
    else:
        u = u_ref[...].astype(F32)
        left, right = _shifted_rows(u, tm)
        conv = cw[0:1] * left + cw[1:2] * u + cw[2:3] * right
    conv_out = (cb_ref[...].astype(F32) * conv).astype(BF16)
    out = _dot(conv_out, w_ref[0:dc, :]) + _dot(rec_ref[...], w_ref[dc:, :])
    o_ref[...] = x_ref[...] + mod_ref[2:3, :] * out


def _out_call(xres, cb, u, rec, w_out, conv_w, mod_l, *, layer, latent, tm, n_tiles, first_block,
              tiles_per_b, mod_index):
    d = xres.shape[1]
    dc = cb.shape[1]
    half = dc // 2
    tile = pl.BlockSpec((tm, dc), lambda i: (i, 0))
    xspec = pl.BlockSpec((tm, d), lambda i: (first_block + i, 0))
    in_specs = [xspec, tile, tile]
    args = [xres, cb, u]
    if latent:
        hb = tm // GRID_W
        last = n_tiles * hb - 1
        in_specs += [
            pl.BlockSpec((GRID_W, half), lambda i: (jnp.maximum(i * hb - 1, 0), 1)),
            pl.BlockSpec((GRID_W, half), lambda i: (jnp.minimum((i + 1) * hb, last), 1)),
        ]
        args += [u, u]
    in_specs += [
        tile,
        pl.BlockSpec((None,) + w_out.shape[1:], lambda i: (layer, 0, 0)),
        pl.BlockSpec(conv_w.shape, lambda i: (0, 0)),
        pl.BlockSpec((None, 6, d), lambda i: (mod_index(i), 0, 0)),
    ]
    args += [rec, w_out, conv_w, mod_l]
    return pl.pallas_call(
        functools.partial(_out_kernel, latent, tiles_per_b),
        grid=(n_tiles,),
        in_specs=in_specs,
        out_specs=xspec,
        out_shape=jax.ShapeDtypeStruct(xres.shape, F32),
        input_output_aliases={0: 0},
        compiler_params=_cparams(1, 56),
        name="mix_out",
    )(*args)


def _swiglu(h, wg_ref, wu_ref, wd_ref, fc):
    f = wg_ref.shape[1]
    acc = jnp.zeros((h.shape[0], wd_ref.shape[1]), F32)
    for f0 in range(0, f, fc):
        f1 = min(f0 + fc, f)
        gate = _dot(h, wg_ref[:, f0:f1])
        up = _dot(h, wu_ref[:, f0:f1])
        act = (gate * _sigmoid(gate) * up).astype(BF16)
        acc = acc + _dot(act, wd_ref[f0:f1, :])
    return acc


def _ffn_kernel(fc, x_ref, mod_ref, g_ref, wg_ref, wu_ref, wd_ref, o_ref):
    x = x_ref[...]
    h = _norm_mod(x, g_ref[...], mod_ref[3:4, :], mod_ref[4:5, :]).astype(BF16)
    o_ref[...] = x + mod_ref[5:6, :] * _swiglu(h, wg_ref, wu_ref, wd_ref, fc)


def _ffn_call(xres, mod_l, norm_g, wg, wu, wd, *, layer, tm, mod_index):
    rows, d = xres.shape
    f = wg.shape[2]
    xspec = pl.BlockSpec((tm, d), lambda i: (i, 0))
    single = pl.Buffered(1)
    return pl.pallas_call(
        functools.partial(_ffn_kernel, FFN_CHUNK),
        grid=(rows // tm,),
        in_specs=[
            xspec,
            pl.BlockSpec((None, 6, d), lambda i: (mod_index(i), 0, 0)),
            pl.BlockSpec((1, d), lambda i: (0, 0)),
            pl.BlockSpec((None, d, f), lambda i: (layer, 0, 0), pipeline_mode=single),
            pl.BlockSpec((None, d, f), lambda i: (layer, 0, 0), pipeline_mode=single),
            pl.BlockSpec((None, f, d), lambda i: (layer, 0, 0), pipeline_mode=single),
        ],
        out_specs=xspec,
        out_shape=jax.ShapeDtypeStruct(xres.shape, F32),
        input_output_aliases={0: 0},
        compiler_params=_cparams(1, 56),
        name="dense_ffn",
    )(xres, mod_l, norm_g, wg, wu, wd)


def _route_kernel(n_experts, x_ref, mod_ref, g_ref, rw_ref, rb_ref,
                  hb_ref, info_ref, infot_ref, cnt_ref):
    tb = x_ref.shape[0]
    h = _norm_mod(x_ref[...], g_ref[...], mod_ref[3:4, :], mod_ref[4:5, :])
    hb_ref[...] = h.astype(BF16)
    logits = _dot3(h, rw_ref[...]) + rb_ref[...]
    lane = lax.broadcasted_iota(jnp.int32, logits.shape, 1)
    logits = jnp.where(lane < n_experts, logits, -jnp.inf)
    m1 = jnp.max(logits, axis=1, keepdims=True)
    i1 = jnp.min(jnp.where(logits == m1, lane, ROUTER_LANES), axis=1, keepdims=True)
    rest = jnp.where(lane == i1, -jnp.inf, logits)
    m2 = jnp.max(rest, axis=1, keepdims=True)
    i2 = jnp.min(jnp.where(rest == m2, lane, ROUTER_LANES), axis=1, keepdims=True)
    e2 = jnp.exp(m2 - m1)
    p1 = 1.0 / (1.0 + e2)
    p2 = e2 * p1

    member = jnp.where((lane == i1) | (lane == i2), 1.0, 0.0)
    earlier = jnp.where(lax.broadcasted_iota(jnp.int32, (tb, tb), 0)
                        > lax.broadcasted_iota(jnp.int32, (tb, tb), 1), 1.0, 0.0)
    rank = _dot(earlier.astype(BF16), member.astype(BF16))
    cnt = jnp.sum(member, axis=0, keepdims=True)
    units = jnp.floor((cnt + (SEG_ALIGN - 1.0)) * (1.0 / SEG_ALIGN))
    before = jnp.where(lax.broadcasted_iota(jnp.int32, (ROUTER_LANES, ROUTER_LANES), 0)
                       < lax.broadcasted_iota(jnp.int32, (ROUTER_LANES, ROUTER_LANES), 1), 1.0, 0.0)
    start = _dot(jnp.broadcast_to(units, (8, ROUTER_LANES)).astype(BF16),
                 before.astype(BF16))[0:1, :] * float(SEG_ALIGN)
    pos = start + rank
    dest1 = jnp.sum(jnp.where(lane == i1, pos, 0.0), axis=1, keepdims=True)
    dest2 = jnp.sum(jnp.where(lane == i2, pos, 0.0), axis=1, keepdims=True)
    info = (jnp.where(lane == 0, dest1, 0.0) + jnp.where(lane == 1, dest2, 0.0)
            + jnp.where(lane == 2, p1, 0.0) + jnp.where(lane == 3, p2, 0.0))
    info_ref[...] = info
    infot_ref[...] = info.T[0:8, :]
    cnt_ref[...] = jnp.broadcast_to(cnt, (8, ROUTER_LANES))


def _route_call(xres, mod_l, norm_g, rw, rb, *, n_blocks, n_experts, mod_index):
    d = xres.shape[1]
    tb = ROUTE_BLOCK
    rows = n_blocks * tb
    return pl.pallas_call(
        functools.partial(_route_kernel, n_experts),
        grid=(n_blocks,),
        in_specs=[
            pl.BlockSpec((tb, d), lambda i: (i, 0)),
            pl.BlockSpec((None, 6, d), lambda i: (mod_index(i), 0, 0)),
            pl.BlockSpec((1, d), lambda i: (0, 0)),
            pl.BlockSpec((d, ROUTER_LANES), lambda i: (0, 0)),
            pl.BlockSpec((1, ROUTER_LANES), lambda i: (0, 0)),
        ],
        out_specs=[
            pl.BlockSpec((tb, d), lambda i: (i, 0)),
            pl.BlockSpec((tb, ROUTER_LANES), lambda i: (i, 0)),
            pl.BlockSpec((None, 8, tb), lambda i: (i, 0, 0)),
            pl.BlockSpec((None, 8, ROUTER_LANES), lambda i: (i, 0, 0)),
        ],
        out_shape=[
            jax.ShapeDtypeStruct((rows, d), BF16),
            jax.ShapeDtypeStruct((rows, ROUTER_LANES), F32),
            jax.ShapeDtypeStruct((n_blocks, 8, tb), F32),
            jax.ShapeDtypeStruct((n_blocks, 8, ROUTER_LANES), F32),
        ],
        compiler_params=_cparams(1, 48),
        name="moe_route",
    )(xres, mod_l, norm_g, rw, rb)


def _segment_dmas(start, n_experts, b, tables, local_ref, global_ref, sem, to_global):
    cntp_sm, loff_sm, dst_sm = tables
    for e in range(n_experts):
        cp = cntp_sm[b * n_experts + e]
        lo = loff_sm[b * n_experts + e]
        go = dst_sm[b * n_experts + e]
        pos = jnp.int32(0)
        size = ROUTE_BLOCK
        while size >= SEG_ALIGN:
            bit = (cp & size) != 0
            loc = local_ref.at[pl.ds(pl.multiple_of(lo + pos, SEG_ALIGN), size)]
            glo = global_ref.at[pl.ds(pl.multiple_of(go + pos, SEG_ALIGN), size)]
            cpy = (pltpu.make_async_copy(loc, glo, sem) if to_global
                   else pltpu.make_async_copy(glo, loc, sem))
            pl.when(bit)(cpy.start if start else cpy.wait)
            pos = pos + jnp.where(bit, size, 0)
            size //= 2


def _sort_kernel(n_experts, cntp_sm, loff_sm, dst_sm, hb_ref, infot_ref, zero_ref, sorted_ref,
                 tile_ref, sem):
    del zero_ref
    b = pl.program_id(0)
    slot = b % 2
    s_loc, tb = tile_ref.shape[1], hb_ref.shape[0]
    d1 = infot_ref[0:1, :].astype(jnp.int32)
    d2 = infot_ref[1:2, :].astype(jnp.int32)
    r = lax.broadcasted_iota(jnp.int32, (s_loc, tb), 0)
    perm = jnp.where((r == d1) | (r == d2), 1.0, 0.0).astype(BF16)
    tile_ref[slot] = _dot(perm, hb_ref[...]).astype(BF16)

    def dmas(start, blk, s):
        _segment_dmas(start, n_experts, blk, (cntp_sm, loff_sm, dst_sm), tile_ref.at[s],
                      sorted_ref, sem.at[s], True)

    @pl.when(b > 0)
    def _():
        dmas(False, b - 1, 1 - slot)

    dmas(True, b, slot)

    @pl.when(b == pl.num_programs(0) - 1)
    def _():
        dmas(False, b, slot)


def _sort_call(cntp, loff, dst, hb, infot, sorted_init, *, n_blocks, n_experts, s_loc):
    d = hb.shape[1]
    tb = ROUTE_BLOCK
    return pl.pallas_call(
        functools.partial(_sort_kernel, n_experts),
        grid_spec=pltpu.PrefetchScalarGridSpec(
            num_scalar_prefetch=3,
            grid=(n_blocks,),
            in_specs=[
                pl.BlockSpec((tb, d), lambda i, *_: (i, 0)),
                pl.BlockSpec((None, 8, tb), lambda i, *_: (i, 0, 0)),
                pl.BlockSpec(memory_space=pl.ANY),
            ],
            out_specs=pl.BlockSpec(memory_space=pl.ANY),
            scratch_shapes=[pltpu.VMEM((2, s_loc, d), BF16), pltpu.SemaphoreType.DMA((2,))],
        ),
        out_shape=jax.ShapeDtypeStruct(sorted_init.shape, BF16),
        input_output_aliases={5: 0},
        compiler_params=_cparams(1, 48),
        name="moe_sort",
    )(cntp, loff, dst, hb, infot, sorted_init)


def _expert_kernel(fc, te_sm, valid_sm, xs_ref, wg_ref, wu_ref, wd_ref, o_ref):
    del te_sm
    valid = valid_sm[pl.program_id(0)] != 0

    @pl.when(valid)
    def _():
        o_ref[...] = _swiglu(xs_ref[...], wg_ref, wu_ref, wd_ref, fc).astype(BF16)

    @pl.when(jnp.logical_not(valid))
    def _():
        o_ref[...] = jnp.zeros(o_ref.shape, BF16)


def _expert_call(tile_expert, tile_valid, xs, wg, wu, wd, *, layer, tmx):
    rows, d = xs.shape
    f = wg.shape[3]
    single = pl.Buffered(1)
    up_spec = pl.BlockSpec((None, None, d, f), lambda i, te, tv: (layer, te[i], 0, 0),
                           pipeline_mode=single)
    return pl.pallas_call(
        functools.partial(_expert_kernel, FFN_CHUNK),
        grid_spec=pltpu.PrefetchScalarGridSpec(
            num_scalar_prefetch=2,
            grid=(rows // tmx,),
            in_specs=[
                pl.BlockSpec((tmx, d), lambda i, te, tv: (i, 0)),
                up_spec,
                up_spec,
                pl.BlockSpec((None, None, f, d), lambda i, te, tv: (layer, te[i], 0, 0),
                             pipeline_mode=single),
            ],
            out_specs=pl.BlockSpec((tmx, d), lambda i, te, tv: (i, 0)),
        ),
        out_shape=jax.ShapeDtypeStruct((rows, d), BF16),
        compiler_params=_cparams(1, 56),
        name="moe_experts",
    )(tile_expert, tile_valid, xs, wg, wu, wd)


def _combine_kernel(n_experts, final, cntp_sm, loff_sm, dst_sm, x_ref, info_ref, mod_ref, *rest):
    if final:
        fg_ref, ys_ref, o_ref, tile_ref, sem = rest
    else:
        ys_ref, o_ref, tile_ref, sem = rest
    b = pl.program_id(0)
    slot = b % 2
    s_loc, tb = tile_ref.shape[1], x_ref.shape[0]

    def dmas(start, blk, s):
        _segment_dmas(start, n_experts, blk, (cntp_sm, loff_sm, dst_sm), tile_ref.at[s],
                      ys_ref, sem.at[s], False)

    def fetch(blk, s):
        tile_ref[s] = jnp.zeros(tile_ref.shape[1:], BF16)
        dmas(True, blk, s)

    @pl.when(b == 0)
    def _():
        fetch(0, 0)

    @pl.when(b + 1 < pl.num_programs(0))
    def _():
        fetch(b + 1, 1 - slot)

    info = info_ref[...]
    d1 = info[:, 0:1].astype(jnp.int32)
    d2 = info[:, 1:2].astype(jnp.int32)
    c = lax.broadcasted_iota(jnp.int32, (tb, s_loc), 1)
    pg = (jnp.where(c == d1, info[:, 2:3], 0.0) + jnp.where(c == d2, info[:, 3:4], 0.0)).astype(BF16)
    dmas(False, b, slot)
    y = _dot(pg, tile_ref[slot])
    xn = x_ref[...] + mod_ref[5:6, :] * y
    if final:
        ms = jnp.mean(xn * xn, axis=-1, keepdims=True)
        xn = xn * lax.rsqrt(ms + EPS) * fg_ref[...]
    o_ref[...] = xn


def _combine_call(cntp, loff, dst, xres, info, mod_l, ys, final_g, *, n_blocks, n_experts,
                  s_loc, mod_index):
    d = xres.shape[1]
    tb = ROUTE_BLOCK
    final = final_g is not None
    xspec = pl.BlockSpec((tb, d), lambda i, *_: (i, 0))
    in_specs = [
        xspec,
        pl.BlockSpec((tb, ROUTER_LANES), lambda i, *_: (i, 0)),
        pl.BlockSpec((None, 6, d), lambda i, *_: (mod_index(i), 0, 0)),
    ]
    args = [xres, info, mod_l]
    if final:
        in_specs.append(pl.BlockSpec((1, d), lambda i, *_: (0, 0)))
        args.append(final_g)
    in_specs.append(pl.BlockSpec(memory_space=pl.ANY))
    args.append(ys)
    out_rows = n_blocks * tb if final else xres.shape[0]
    return pl.pallas_call(
        functools.partial(_combine_kernel, n_experts, final),
        grid_spec=pltpu.PrefetchScalarGridSpec(
            num_scalar_prefetch=3,
            grid=(n_blocks,),
            in_specs=in_specs,
            out_specs=xspec,
            scratch_shapes=[pltpu.VMEM((2, s_loc, d), BF16), pltpu.SemaphoreType.DMA((2,))],
        ),
        out_shape=jax.ShapeDtypeStruct((out_rows, d), F32),
        input_output_aliases={} if final else {3: 0},
        compiler_params=_cparams(1, 48),
        name="moe_combine",
    )(cntp, loff, dst, *args)


def _moe_layout(cnt, n_tiles, tmx):
    cntp = (cnt + (SEG_ALIGN - 1)) // SEG_ALIGN * SEG_ALIGN
    loff = jnp.cumsum(cntp, axis=1) - cntp
    total = jnp.sum(cntp, axis=0)
    region = (total + (tmx - 1)) // tmx * tmx
    region_end = jnp.cumsum(region)
    dst = (region_end - region)[None, :] + jnp.cumsum(cntp, axis=0) - cntp
    tile_start = jnp.arange(n_tiles, dtype=jnp.int32) * tmx
    valid = tile_start < region_end[-1]
    last_tile = jnp.maximum(region_end[-1] // tmx - 1, 0) * tmx
    start_eff = jnp.where(valid, tile_start, last_tile)
    owner = jnp.sum((start_eff[:, None] >= region_end[None, :]).astype(jnp.int32), axis=1)
    flat = lambda v: v.reshape(-1).astype(jnp.int32)
    return flat(cntp), flat(loff), flat(dst), owner.astype(jnp.int32), valid.astype(jnp.int32)


def _moe_call(xres, mod_l, norm_g, rw, rb, wg, wu, wd, final_g, *, layer, n_blocks, mod_index):
    d = xres.shape[1]
    n_experts = wg.shape[1]
    tmx = EXPERT_ROWS
    block_rows = 2 * ROUTE_BLOCK + n_experts * (SEG_ALIGN - 1)
    s_loc = -(-block_rows // LANES) * LANES
    max_rows = n_blocks * block_rows + n_experts * (tmx - SEG_ALIGN)
    n_tiles = max_rows // tmx
    hb, info, infot, cnt = _route_call(xres, mod_l, norm_g, rw, rb, n_blocks=n_blocks,
                                       n_experts=n_experts, mod_index=mod_index)
    cnt = cnt[:, 0, :n_experts].astype(jnp.int32)
    cntp, loff, dst, owner, valid = _moe_layout(cnt, n_tiles, tmx)
    xs = _sort_call(cntp, loff, dst, hb, infot, jnp.zeros((n_tiles * tmx, d), BF16),
                    n_blocks=n_blocks, n_experts=n_experts, s_loc=s_loc)
    ys = _expert_call(owner, valid, xs, wg, wu, wd, layer=layer, tmx=tmx)
    return _combine_call(cntp, loff, dst, xres, info, mod_l, ys, final_g, n_blocks=n_blocks,
                         n_experts=n_experts, s_loc=s_loc, mod_index=mod_index)


def _final_kernel(x_ref, g_ref, o_ref):
    x = x_ref[...]
    ms = jnp.mean(x * x, axis=-1, keepdims=True)
    o_ref[...] = x * lax.rsqrt(ms + EPS) * g_ref[...]


def _final_call(xres, g, *, rows, tm):
    d = xres.shape[1]
    spec = pl.BlockSpec((tm, d), lambda i: (i, 0))
    return pl.pallas_call(
        _final_kernel,
        grid=(rows // tm,),
        in_specs=[spec, pl.BlockSpec((1, d), lambda i: (0, 0))],
        out_specs=spec,
        out_shape=jax.ShapeDtypeStruct((rows, d), F32),
        compiler_params=_cparams(1, 48),
        name="final_norm",
    )(xres, g)


def kernel(x, c, ctx, c_ctx, ada_w, ada_b, norm_mix_g, norm_ffn_g, w_in, w_out, conv_w,
           rec_conv_w, rec_conv_b, lru_wa, lru_ba, lru_wi, lru_bi, lru_lam,
           dense_wg, dense_wu, dense_wd, router_w, router_b, moe_wg, moe_wu, moe_wd, final_g):
    batch, n_lat, d = x.shape
    n_ctx = ctx.shape[1]
    depth = ada_w.shape[0]
    n_experts = router_w.shape[2]
    rows_lat = batch * n_lat
    rows_ctx = batch * n_ctx
    tm = min(512, n_lat)
    tiles_per_b = n_lat // tm
    ctx_block0 = rows_lat // n_ctx
    tb = ROUTE_BLOCK
    assert n_lat % tb == 0 and rows_ctx % tb == 0 and batch < 8

    xres = jnp.concatenate([x.reshape(rows_lat, d), ctx.reshape(rows_ctx, d)], axis=0)
    cvec =jnp.zeros((8, d), F32).at[:batch].set(c).at[batch].set(c_ctx)
    mod = _ada_call(cvec, ada_w, ada_b).reshape(depth, 8, 6, d)

    def lat_index(i):
        return i // tiles_per_b

    def ctx_index(i):
        return batch

    def all_index(i):
        return jnp.where(i < rows_lat // tb, (i * tb) // n_lat, batch)

    rw = jnp.pad(router_w, ((0, 0), (0, 0), (0, ROUTER_LANES - n_experts)))
    rb = jnp.pad(router_b, ((0, 0), (0, ROUTER_LANES - n_experts)))[:, None, :]

    w_in_b = w_in.astype(BF16)
    w_out_b = w_out.astype(BF16)
    dense_b = [w.astype(BF16) for w in (dense_wg, dense_wu, dense_wd)]
    moe_b = [w.astype(BF16) for w in (moe_wg, moe_wu, moe_wd)]

    out = None
    for l in range(depth):
        last = l == depth - 1
        mod_l = mod[l]
        gmix = norm_mix_g[l][None, :]
        gffn = norm_ffn_g[l][None, :]

        cb_l, u_l, rx_l, g_l = _proj_call(xres, mod_l, gmix, w_in_b, layer=l, tm=tm,
                                          n_tiles=rows_lat // tm, first_block=0,
                                          mod_index=lat_index)
        cb_c, u_c, rx_c, g_c = _proj_call(xres, mod_l, gmix, w_in_b, layer=l, tm=n_ctx,
                                          n_tiles=batch, first_block=ctx_block0,
                                          mod_index=ctx_index)
        wgate, bgate, lam_b = _gate_params(lru_wa[l], lru_ba[l], lru_wi[l], lru_bi[l], lru_lam[l])
        rec_l, rec_c = _scan_call(rx_l, rx_c, g_l, g_c, rec_conv_w[l], rec_conv_b[l][None, :],
                                  wgate, bgate, lam_b, batch=batch)
        xres = _out_call(xres, cb_l, u_l, rec_l, w_out_b, conv_w[l], mod_l, layer=l, latent=True,
                         tm=tm, n_tiles=rows_lat // tm, first_block=0, tiles_per_b=tiles_per_b,
                         mod_index=lat_index)
        if not last:
            xres = _out_call(xres, cb_c, u_c, rec_c, w_out_b, conv_w[l], mod_l, layer=l,
                             latent=False, tm=n_ctx, n_tiles=batch, first_block=ctx_block0,
                             tiles_per_b=1, mod_index=ctx_index)

        j = l // 2
        if l % 2 == 0:
            xres = _ffn_call(xres, mod_l, gffn, *dense_b, layer=j, tm=tb, mod_index=all_index)
        else:
            n_blocks = rows_lat // tb if last else (rows_lat + rows_ctx) // tb
            res = _moe_call(xres, mod_l, gffn, rw[j], rb[j], *moe_b,
                            final_g[None, :] if last else None, layer=j, n_blocks=n_blocks,
                            mod_index=all_index)
            if last:
                out = res
            else:
                xres = res

    if out is None:
        out = _final_call(xres, final_g[None, :], rows=rows_lat, tm=tm)
    return out.reshape(batch, n_lat, d)
```

```python
import functools
import math

import jax
import jax.numpy as jnp
from jax import lax
from jax.experimental import pallas as pl
from jax.experimental.pallas import tpu as pltpu

GRID_W = 64
LRU_C = 8.0
EPS = 1e-6
REC_HEAD_DIM = 64

LANES = 128
SEG = 128
SEG_PITCH = SEG + 8
CTX_NSEG = 8
MAX_GROUP = 64
GATE_SEGS = 4
SCAN_UNROLL = 8
ROUTER_LANES = 128
ROUTE_BLOCK = 512
SEG_ALIGN = 16
EXPERT_ROWS = 512
FFN_CHUNK = 512
MIX_ROWS = 1024

F32 = jnp.float32
BF16 = jnp.bfloat16


def _cparams(n_grid, vmem_mb):
    return pltpu.CompilerParams(
        dimension_semantics=("arbitrary",) * n_grid,
        vmem_limit_bytes=vmem_mb * 1024 * 1024)


def _dot(a, b):
    return jnp.dot(a, b, preferred_element_type=F32)


def _sigmoid(z):
    return 1.0 / (1.0 + jnp.exp2(z * (-math.log2(math.e))))


def _split_bf16(v):
    hi = v.astype(BF16)
    lo = (v - hi.astype(F32)).astype(BF16)
    return hi, lo


def _dot3(a, b):
    a_hi, a_lo = _split_bf16(a)
    b_hi, b_lo = _split_bf16(b)
    return _dot(a_hi, b_hi) + _dot(a_hi, b_lo) + _dot(a_lo, b_hi)


def _norm_mod(x, g, shift, scale):
    ms = jnp.mean(x * x, axis=-1, keepdims=True)
    y = x * lax.rsqrt(ms + EPS) * g
    return y * (1.0 + scale) + shift


def _ada_kernel(c_ref, w_ref, b_ref, o_ref):
    c = c_ref[...]
    s = c * _sigmoid(c)
    o_ref[...] = _dot3(s, w_ref[...]) + b_ref[...]


def _ada_call(cvec, ada_w, ada_b):
    depth, d, d6 = ada_w.shape
    tn = d6 // 4
    return pl.pallas_call(
        _ada_kernel,
        grid=(depth, d6 // tn),
        in_specs=[
            pl.BlockSpec((8, d), lambda l, j: (0, 0)),
            pl.BlockSpec((None, d, tn), lambda l, j: (l, 0, j)),
            pl.BlockSpec((None, 1, tn), lambda l, j: (l, 0, j)),
        ],
        out_specs=pl.BlockSpec((None, 8, tn), lambda l, j: (l, 0, j)),
        out_shape=jax.ShapeDtypeStruct((depth, 8, d6), F32),
        compiler_params=_cparams(2, 48),
        name="ada_mod",
    )(cvec, ada_w, ada_b.reshape(depth, 1, d6))


def _proj_kernel(dh, x_ref, mod_ref, g_ref, w_ref, cb_ref, u_ref, rx_ref, gl_ref):
    h = _norm_mod(x_ref[...], g_ref[...], mod_ref[0:1, :], mod_ref[1:2, :]).astype(BF16)

    def proj(j):
        return _dot(h, w_ref[:, j * dh:(j + 1) * dh])

    cb_ref[...] = proj(0).astype(BF16)
    u_ref[...] = (proj(1) * proj(2)).astype(BF16)
    rx_ref[...] = proj(3).astype(BF16)
    rg = proj(4)
    gelu = 0.5 * rg * (1.0 + jnp.tanh(math.sqrt(2.0 / math.pi) * (rg + 0.044715 * rg * rg * rg)))
    gl_ref[...] = gelu.astype(BF16)


def _proj_call(xres, mod_l, norm_g, w_in, *, layer, tm, n_tiles, first_block, mod_index):
    d = xres.shape[1]
    d_in = w_in.shape[2]
    dh = d_in // 5
    out = jax.ShapeDtypeStruct((n_tiles * tm, dh), BF16)
    ospec = pl.BlockSpec((tm, dh), lambda i: (i, 0))
    return pl.pallas_call(
        functools.partial(_proj_kernel, dh),
        grid=(n_tiles,),
        in_specs=[
            pl.BlockSpec((tm, d), lambda i: (first_block + i, 0)),
            pl.BlockSpec((None, 6, d), lambda i: (mod_index(i), 0, 0)),
            pl.BlockSpec((1, d), lambda i: (0, 0)),
            pl.BlockSpec((None, d, d_in), lambda i: (layer, 0, 0)),
        ],
        out_specs=[ospec, ospec, ospec, ospec],
        out_shape=[out, out, out, out],
        compiler_params=_cparams(1, 56),
        name="mix_proj",
    )(xres, mod_l, norm_g, w_in)


def _scan_kernel(n_lat, n_ctx, rxl_ref, rxc_ref, gl_ref, gc_ref, cw_ref, cbias_ref, wg_ref,
                 gb_ref, lam_ref, recl_ref, recc_ref, xpl_ref, xpc_ref, ab_ref, cin_ref):
    nseg_l = n_lat // SEG
    seg_c = n_ctx // CTX_NSEG
    pitch_c = seg_c + 8
    base_l = CTX_NSEG * pitch_c
    group = min(MAX_GROUP, nseg_l)
    n_groups = nseg_l // group
    gsegs = min(GATE_SEGS, nseg_l)

    def stage(src_ref, dst_ref, n):
        dst_ref[0:8, :] = jnp.zeros((8, LANES), F32)
        dst_ref[n + 8:n + 16, :] = jnp.zeros((8, LANES), F32)
        rows = min(512, n)

        def body(k, _):
            r0 = pl.multiple_of(k * rows, rows)
            dst_ref[pl.ds(r0 + 8, rows), :] = src_ref[pl.ds(r0, rows), :].astype(F32)
            return 0

        lax.fori_loop(0, n // rows, body, 0)

    stage(rxl_ref, xpl_ref, n_lat)
    stage(rxc_ref, xpc_ref, n_ctx)

    lam = lam_ref[...]
    softplus = jnp.maximum(-lam, 0.0) + jnp.log1p(jnp.exp(-jnp.abs(lam)))
    cvec2 = (-LRU_C * math.log2(math.e)) * softplus
    cw = cw_ref[...]
    cbias = cbias_ref[...]
    gb = gb_ref[...]

    def gates(xp_ref, t0, nsegs, seg_len, dst0, pitch):
        nrows = nsegs * seg_len
        win = xp_ref[pl.ds(t0, nrows + 16), :]
        xr = (cw[0:1] * win[6:6 + nrows] + cw[1:2] * win[7:7 + nrows]
              + cw[2:3] * win[8:8 + nrows] + cw[3:4] * win[9:9 + nrows] + cbias)
        z = _dot(xr.astype(BF16), wg_ref[...]) + gb
        for d in range(2):
            r = _sigmoid(z[:, 2 * d * LANES:(2 * d + 1) * LANES])
            gi = _sigmoid(z[:, (2 * d + 1) * LANES:(2 * d + 2) * LANES])
            a = jnp.exp2(cvec2[d:d + 1] * r)
            y = (1.0 - a) * (1.0 + a)
            b = jnp.where(y > 0.0, y * lax.rsqrt(y), 0.0) * (gi * xr)
            for s in range(nsegs):
                rows = pl.ds(dst0 + s * pitch, seg_len)
                ab_ref[2 * d, rows, :] = a[s * seg_len:(s + 1) * seg_len]
                ab_ref[2 * d + 1, rows, :] = b[s * seg_len:(s + 1) * seg_len]

    gates(xpc_ref, 0, CTX_NSEG, seg_c, 0, pitch_c)

    def gates_lat(j, _):
        gates(xpl_ref, pl.multiple_of(j * (gsegs * SEG), gsegs * SEG), gsegs, SEG,
              pl.multiple_of(base_l + j * (gsegs * SEG_PITCH), 8), SEG_PITCH)
        return 0

    lax.fori_loop(0, nseg_l // gsegs, gates_lat, 0)

    def scan_group(d, base, nseg, seg_len, pitch, seg0, carry, reverse):
        a_ref = ab_ref.at[2 * d]
        b_ref = ab_ref.at[2 * d + 1]

        def rows(k):
            s = (seg_len - 1 - k) if reverse else k
            return pl.ds(base + s, nseg, stride=pitch)

        def summarise(k, hc):
            h, acc = hc
            a = a_ref[rows(k), :]
            return a * h + b_ref[rows(k), :], a * acc

        h_end, a_end = lax.fori_loop(
            0, seg_len, summarise,
            (jnp.zeros((nseg, LANES), F32), jnp.ones((nseg, LANES), F32)), unroll=SCAN_UNROLL)
        order = range(nseg - 1, -1, -1) if reverse else range(nseg)
        for j in order:
            cin_ref[seg0 + j:seg0 + j + 1, :] = carry
            carry = h_end[j:j + 1] + a_end[j:j + 1] * carry

        def write(k, h):
            h = a_ref[rows(k), :] * h + b_ref[rows(k), :]
            b_ref[rows(k), :] = h
            return h

        lax.fori_loop(0, seg_len, write, cin_ref[seg0:seg0 + nseg, :], unroll=SCAN_UNROLL)
        return carry

    for d, reverse in ((0, False), (1, True)):
        carry = jnp.zeros((1, LANES), F32)
        carry = scan_group(d, 0, CTX_NSEG, seg_c, pitch_c, 0, carry, reverse)
        groups = range(n_groups - 1, -1, -1) if reverse else range(n_groups)
        for gidx in groups:
            carry = scan_group(d, base_l + gidx * group * SEG_PITCH, group, SEG, SEG_PITCH,
                               CTX_NSEG + gidx * group, carry, reverse)

    def emit(g_ref, out_ref, t0, nsegs, seg_len, src0, pitch):
        for s in range(nsegs):
            src = pl.ds(src0 + s * pitch, seg_len)
            rows = pl.ds(t0 + s * seg_len, seg_len)
            h = ab_ref[1, src, :] + ab_ref[3, src, :]
            out_ref[rows, :] = (h * g_ref[rows, :].astype(F32)).astype(BF16)

    emit(gc_ref, recc_ref, 0, CTX_NSEG, seg_c, 0, pitch_c)

    def emit_lat(j, _):
        emit(gl_ref, recl_ref, pl.multiple_of(j * (gsegs * SEG), gsegs * SEG), gsegs, SEG,
             pl.multiple_of(base_l + j * (gsegs * SEG_PITCH), 8), SEG_PITCH)
        return 0

    lax.fori_loop(0, nseg_l // gsegs, emit_lat, 0)


def _scan_call(rx_l, rx_c, g_l, g_c, conv_w, conv_b, wgate, bgate, lam, *, batch):
    n_lat = rx_l.shape[0] // batch
    n_ctx = rx_c.shape[0] // batch
    d_rec = rx_l.shape[1]
    ncb = d_rec // LANES
    seg_c = n_ctx // CTX_NSEG
    ab_rows = CTX_NSEG * (seg_c + 8) + (n_lat // SEG) * SEG_PITCH
    nseg_tot = CTX_NSEG + n_lat // SEG
    lat = pl.BlockSpec((n_lat, LANES), lambda b, c: (b, c))
    ctx = pl.BlockSpec((n_ctx, LANES), lambda b, c: (b, c))
    return pl.pallas_call(
        functools.partial(_scan_kernel, n_lat, n_ctx),
        grid=(batch, ncb),
        in_specs=[
            lat, ctx, lat, ctx,
            pl.BlockSpec((4, LANES), lambda b, c: (0, c)),
            pl.BlockSpec((1, LANES), lambda b, c: (0, c)),
            pl.BlockSpec((None, LANES, 4 * LANES), lambda b, c: (c, 0, 0)),
            pl.BlockSpec((None, 1, 4 * LANES), lambda b, c: (c, 0, 0)),
            pl.BlockSpec((None, 2, LANES), lambda b, c: (c, 0, 0)),
        ],
        out_specs=[lat, ctx],
        out_shape=[jax.ShapeDtypeStruct(rx_l.shape, BF16),
                   jax.ShapeDtypeStruct(rx_c.shape, BF16)],
        scratch_shapes=[
            pltpu.VMEM((n_lat + 16, LANES), F32),
            pltpu.VMEM((n_ctx + 16, LANES), F32),
            pltpu.VMEM((4, ab_rows, LANES), F32),
            pltpu.VMEM((nseg_tot, LANES), F32),
        ],
        compiler_params=_cparams(2, 56),
        name="rglru_scan",
    )(rx_l, rx_c, g_l, g_c, conv_w, conv_b, wgate, bgate, lam)


def _gate_params(wa, ba, wi, bi, lam):
    n_heads = wa.shape[1]
    hpb = LANES // REC_HEAD_DIM
    ncb = n_heads // hpb

    def blockdiag(w):
        w = w.reshape(ncb, hpb, REC_HEAD_DIM, REC_HEAD_DIM)
        rows = []
        for p in range(hpb):
            cols = [w[:, p] if q == p else jnp.zeros_like(w[:, p]) for q in range(hpb)]
            rows.append(jnp.concatenate(cols, axis=2))
        return jnp.concatenate(rows, axis=1)

    wgate = jnp.concatenate(
        [blockdiag(wa[0]), blockdiag(wi[0]), blockdiag(wa[1]), blockdiag(wi[1])], axis=2)

    def per_block(v):
        return v.reshape(ncb, 1, LANES)

    bgate = jnp.concatenate(
        [per_block(ba[0]), per_block(bi[0]), per_block(ba[1]), per_block(bi[1])], axis=2)
    lam_b = jnp.transpose(lam.reshape(2, ncb, LANES), (1, 0, 2))
    return wgate.astype(BF16), bgate, lam_b


def _shifted_rows(u, period):
    n = u.shape[0]
    col = lax.broadcasted_iota(jnp.int32, u.shape, 0) % period
    left = jnp.where(col == 0, 0.0, pltpu.roll(u, 1, 0))
    right = jnp.where(col == period - 1, 0.0, pltpu.roll(u, n - 1, 0))
    return left, right


def _out_kernel(latent, tiles_per_b, *refs):
    if latent:
        x_ref, cb_ref, u_ref, up_ref, un_ref, rec_ref, w_ref, cw_ref, mod_ref, o_ref = refs
    else:
        x_ref, cb_ref, u_ref, rec_ref, w_ref, cw_ref, mod_ref, o_ref = refs
    tm, dc = u_ref.shape
    cw = cw_ref[...]
    if latent:
        half = dc // 2
        uh = u_ref[:, 0:half].astype(F32)
        left, right = _shifted_rows(uh, GRID_W)
        conv_h = cw[0:1, 0:half] * left + cw[1:2, 0:half] * uh + cw[2:3, 0:half] * right
        uv = u_ref[:, half:dc].astype(F32)
        t = pl.program_id(0) % tiles_per_b
        prev = jnp.where(t > 0, up_ref[...].astype(F32), 0.0)
        nxt = jnp.where(t < tiles_per_b - 1, un_ref[...].astype(F32), 0.0)
        up = jnp.concatenate([prev, uv[0:tm - GRID_W]], axis=0)
        down = jnp.concatenate([uv[GRID_W:tm], nxt], axis=0)
        conv_v = cw[0:1, half:dc] * up + cw[1:2, half:dc] * uv + cw[2:3, half:dc] * down
        conv = jnp.concatenate([conv_h, conv_v], axis=1)
    else:
        u = u_ref[...].astype(F32)
        left, right = _shifted_rows(u, tm)
        conv = cw[0:1] * left + cw[1:2] * u + cw[2:3] * right
    conv_out = (cb_ref[...].astype(F32) * conv).astype(BF16)
    out = _dot(conv_out, w_ref[0:dc, :]) + _dot(rec_ref[...], w_ref[dc:, :])
    o_ref[...] = x_ref[...] + mod_ref[2:3, :] * out


def _out_call(xres, cb, u, rec, w_out, conv_w, mod_l, *, layer, latent, tm, n_tiles, first_block,
              tiles_per_b, mod_index):
    d = xres.shape[1]
    dc = cb.shape[1]
    half = dc // 2
    tile = pl.BlockSpec((tm, dc), lambda i: (i, 0))
    xspec = pl.BlockSpec((tm, d), lambda i: (first_block + i, 0))
    in_specs = [xspec, tile, tile]
    args = [xres, cb, u]
    if latent:
        hb = tm // GRID_W
        last = n_tiles * hb - 1
        in_specs += [
            pl.BlockSpec((GRID_W, half), lambda i: (jnp.maximum(i * hb - 1, 0), 1)),
            pl.BlockSpec((GRID_W, half), lambda i: (jnp.minimum((i + 1) * hb, last), 1)),
        ]
        args += [u, u]
    in_specs += [
        tile,
        pl.BlockSpec((None,) + w_out.shape[1:], lambda i: (layer, 0, 0)),
        pl.BlockSpec(conv_w.shape, lambda i: (0, 0)),
        pl.BlockSpec((None, 6, d), lambda i: (mod_index(i), 0, 0)),
    ]
    args += [rec, w_out, conv_w, mod_l]
    return pl.pallas_call(
        functools.partial(_out_kernel, latent, tiles_per_b),
        grid=(n_tiles,),
        in_specs=in_specs,
        out_specs=xspec,
        out_shape=jax.ShapeDtypeStruct(xres.shape, F32),
        input_output_aliases={0: 0},
        compiler_params=_cparams(1, 56),
        name="mix_out",
    )(*args)


def _swiglu(h, wg_ref, wu_ref, wd_ref, fc):
    f = wg_ref.shape[1]
    acc = jnp.zeros((h.shape[0], wd_ref.shape[1]), F32)
    for f0 in range(0, f, fc):
        f1 = min(f0 + fc, f)
        gate = _dot(h, wg_ref[:, f0:f1])
        up = _dot(h, wu_ref[:, f0:f1])
        act = (gate * _sigmoid(gate) * up).astype(BF16)
        acc = acc + _dot(act, wd_ref[f0:f1, :])
    return acc


def _ffn_kernel(fc, x_ref, mod_ref, g_ref, wg_ref, wu_ref, wd_ref, o_ref):
    x = x_ref[...]
    h = _norm_mod(x, g_ref[...], mod_ref[3:4, :], mod_ref[4:5, :]).astype(BF16)
    o_ref[...] = x + mod_ref[5:6, :] * _swiglu(h, wg_ref, wu_ref, wd_ref, fc)


def _ffn_call(xres, mod_l, norm_g, wg, wu, wd, *, layer, tm, mod_index):
    rows, d = xres.shape
    f = wg.shape[2]
    xspec = pl.BlockSpec((tm, d), lambda i: (i, 0))
    single = pl.Buffered(1)
    return pl.pallas_call(
        functools.partial(_ffn_kernel, FFN_CHUNK),
        grid=(rows // tm,),
        in_specs=[
            xspec,
            pl.BlockSpec((None, 6, d), lambda i: (mod_index(i), 0, 0)),
            pl.BlockSpec((1, d), lambda i: (0, 0)),
            pl.BlockSpec((None, d, f), lambda i: (layer, 0, 0), pipeline_mode=single),
            pl.BlockSpec((None, d, f), lambda i: (layer, 0, 0), pipeline_mode=single),
            pl.BlockSpec((None, f, d), lambda i: (layer, 0, 0), pipeline_mode=single),
        ],
        out_specs=xspec,
        out_shape=jax.ShapeDtypeStruct(xres.shape, F32),
        input_output_aliases={0: 0},
        compiler_params=_cparams(1, 56),
        name="dense_ffn",
    )(xres, mod_l, norm_g, wg, wu, wd)


def _route_kernel(n_experts, x_ref, mod_ref, g_ref, rw_ref, rb_ref,
                  hb_ref, info_ref, infot_ref, cnt_ref):
    tb = x_ref.shape[0]
    h = _norm_mod(x_ref[...], g_ref[...], mod_ref[3:4, :], mod_ref[4:5, :])
    hb_ref[...] = h.astype(BF16)
    logits = _dot3(h, rw_ref[...]) + rb_ref[...]
    lane = lax.broadcasted_iota(jnp.int32, logits.shape, 1)
    logits = jnp.where(lane < n_experts, logits, -jnp.inf)
    m1 = jnp.max(logits, axis=1, keepdims=True)
    i1 = jnp.min(jnp.where(logits == m1, lane, ROUTER_LANES), axis=1, keepdims=True)
    rest = jnp.where(lane == i1, -jnp.inf, logits)
    m2 = jnp.max(rest, axis=1, keepdims=True)
    i2 = jnp.min(jnp.where(rest == m2, lane, ROUTER_LANES), axis=1, keepdims=True)
    e2 = jnp.exp(m2 - m1)
    p1 = 1.0 / (1.0 + e2)
    p2 = e2 * p1

    member = jnp.where((lane == i1) | (lane == i2), 1.0, 0.0)
    earlier = jnp.where(lax.broadcasted_iota(jnp.int32, (tb, tb), 0)
                        > lax.broadcasted_iota(jnp.int32, (tb, tb), 1), 1.0, 0.0)
    rank = _dot(earlier.astype(BF16), member.astype(BF16))
    cnt = jnp.sum(member, axis=0, keepdims=True)
    units = jnp.floor((cnt + (SEG_ALIGN - 1.0)) * (1.0 / SEG_ALIGN))
    before = jnp.where(lax.broadcasted_iota(jnp.int32, (ROUTER_LANES, ROUTER_LANES), 0)
                       < lax.broadcasted_iota(jnp.int32, (ROUTER_LANES, ROUTER_LANES), 1), 1.0, 0.0)
    start = _dot(jnp.broadcast_to(units, (8, ROUTER_LANES)).astype(BF16),
                 before.astype(BF16))[0:1, :] * float(SEG_ALIGN)
    pos = start + rank
    dest1 = jnp.sum(jnp.where(lane == i1, pos, 0.0), axis=1, keepdims=True)
    dest2 = jnp.sum(jnp.where(lane == i2, pos, 0.0), axis=1, keepdims=True)
    info = (jnp.where(lane == 0, dest1, 0.0) + jnp.where(lane == 1, dest2, 0.0)
            + jnp.where(lane == 2, p1, 0.0) + jnp.where(lane == 3, p2, 0.0))
    info_ref[...] = info
    infot_ref[...] = info.T[0:8, :]
    cnt_ref[...] = jnp.broadcast_to(cnt, (8, ROUTER_LANES))


def _route_call(xres, mod_l, norm_g, rw, rb, *, n_blocks, n_experts, mod_index):
    d = xres.shape[1]
    tb = ROUTE_BLOCK
    rows = n_blocks * tb
    return pl.pallas_call(
        functools.partial(_route_kernel, n_experts),
        grid=(n_blocks,),
        in_specs=[
            pl.BlockSpec((tb, d), lambda i: (i, 0)),
            pl.BlockSpec((None, 6, d), lambda i: (mod_index(i), 0, 0)),
            pl.BlockSpec((1, d), lambda i: (0, 0)),
            pl.BlockSpec((d, ROUTER_LANES), lambda i: (0, 0)),
            pl.BlockSpec((1, ROUTER_LANES), lambda i: (0, 0)),
        ],
        out_specs=[
            pl.BlockSpec((tb, d), lambda i: (i, 0)),
            pl.BlockSpec((tb, ROUTER_LANES), lambda i: (i, 0)),
            pl.BlockSpec((None, 8, tb), lambda i: (i, 0, 0)),
            pl.BlockSpec((None, 8, ROUTER_LANES), lambda i: (i, 0, 0)),
        ],
        out_shape=[
            jax.ShapeDtypeStruct((rows, d), BF16),
            jax.ShapeDtypeStruct((rows, ROUTER_LANES), F32),
            jax.ShapeDtypeStruct((n_blocks, 8, tb), F32),
            jax.ShapeDtypeStruct((n_blocks, 8, ROUTER_LANES), F32),
        ],
        compiler_params=_cparams(1, 48),
        name="moe_route",
    )(xres, mod_l, norm_g, rw, rb)


def _segment_dmas(start, n_experts, b, tables, local_ref, global_ref, sem, to_global):
    cntp_sm, loff_sm, dst_sm = tables
    for e in range(n_experts):
        cp = cntp_sm[b * n_experts + e]
        lo = loff_sm[b * n_experts + e]
        go = dst_sm[b * n_experts + e]
        pos = jnp.int32(0)
        size = ROUTE_BLOCK
        while size >= SEG_ALIGN:
            bit = (cp & size) != 0
            loc = local_ref.at[pl.ds(pl.multiple_of(lo + pos, SEG_ALIGN), size)]
            glo = global_ref.at[pl.ds(pl.multiple_of(go + pos, SEG_ALIGN), size)]
            cpy = (pltpu.make_async_copy(loc, glo, sem) if to_global
                   else pltpu.make_async_copy(glo, loc, sem))
            pl.when(bit)(cpy.start if start else cpy.wait)
            pos = pos + jnp.where(bit, size, 0)
            size //= 2


def _sort_kernel(n_experts, cntp_sm, loff_sm, dst_sm, hb_ref, infot_ref, zero_ref, sorted_ref,
                 tile_ref, sem):
    del zero_ref
    b = pl.program_id(0)
    slot = b % 2
    s_loc, tb = tile_ref.shape[1], hb_ref.shape[0]
    d1 = infot_ref[0:1, :].astype(jnp.int32)
    d2 = infot_ref[1:2, :].astype(jnp.int32)
    r = lax.broadcasted_iota(jnp.int32, (s_loc, tb), 0)
    perm = jnp.where((r == d1) | (r == d2), 1.0, 0.0).astype(BF16)
    tile_ref[slot] = _dot(perm, hb_ref[...]).astype(BF16)

    def dmas(start, blk, s):
        _segment_dmas(start, n_experts, blk, (cntp_sm, loff_sm, dst_sm), tile_ref.at[s],
                      sorted_ref, sem.at[s], True)

    @pl.when(b > 0)
    def _():
        dmas(False, b - 1, 1 - slot)

    dmas(True, b, slot)

    @pl.when(b == pl.num_programs(0) - 1)
    def _():
        dmas(False, b, slot)


def _sort_call(cntp, loff, dst, hb, infot, sorted_init, *, n_blocks, n_experts, s_loc):
    d = hb.shape[1]
    tb = ROUTE_BLOCK
    return pl.pallas_call(
        functools.partial(_sort_kernel, n_experts),
        grid_spec=pltpu.PrefetchScalarGridSpec(
            num_scalar_prefetch=3,
            grid=(n_blocks,),
            in_specs=[
                pl.BlockSpec((tb, d), lambda i, *_: (i, 0)),
                pl.BlockSpec((None, 8, tb), lambda i, *_: (i, 0, 0)),
                pl.BlockSpec(memory_space=pl.ANY),
            ],
            out_specs=pl.BlockSpec(memory_space=pl.ANY),
            scratch_shapes=[pltpu.VMEM((2, s_loc, d), BF16), pltpu.SemaphoreType.DMA((2,))],
        ),
        out_shape=jax.ShapeDtypeStruct(sorted_init.shape, BF16),
        input_output_aliases={5: 0},
        compiler_params=_cparams(1, 48),
        name="moe_sort",
    )(cntp, loff, dst, hb, infot, sorted_init)


def _expert_kernel(fc, te_sm, valid_sm, xs_ref, wg_ref, wu_ref, wd_ref, o_ref):
    del te_sm
    valid = valid_sm[pl.program_id(0)] != 0

    @pl.when(valid)
    def _():
        o_ref[...] = _swiglu(xs_ref[...], wg_ref, wu_ref, wd_ref, fc).astype(BF16)

    @pl.when(jnp.logical_not(valid))
    def _():
        o_ref[...] = jnp.zeros(o_ref.shape, BF16)


def _expert_call(tile_expert, tile_valid, xs, wg, wu, wd, *, layer, tmx):
    rows, d = xs.shape
    f = wg.shape[3]
    single = pl.Buffered(1)
    up_spec = pl.BlockSpec((None, None, d, f), lambda i, te, tv: (layer, te[i], 0, 0),
                           pipeline_mode=single)
    return pl.pallas_call(
        functools.partial(_expert_kernel, FFN_CHUNK),
        grid_spec=pltpu.PrefetchScalarGridSpec(
            num_scalar_prefetch=2,
            grid=(rows // tmx,),
            in_specs=[
                pl.BlockSpec((tmx, d), lambda i, te, tv: (i, 0)),
                up_spec,
                up_spec,
                pl.BlockSpec((None, None, f, d), lambda i, te, tv: (layer, te[i], 0, 0),
                             pipeline_mode=single),
            ],
            out_specs=pl.BlockSpec((tmx, d), lambda i, te, tv: (i, 0)),
        ),
        out_shape=jax.ShapeDtypeStruct((rows, d), BF16),
        compiler_params=_cparams(1, 56),
        name="moe_experts",
    )(tile_expert, tile_valid, xs, wg, wu, wd)


def _combine_kernel(n_experts, final, cntp_sm, loff_sm, dst_sm, x_ref, info_ref, mod_ref, *rest):
    if final:
        fg_ref, ys_ref, o_ref, tile_ref, sem = rest
    else:
        ys_ref, o_ref, tile_ref, sem = rest
    b = pl.program_id(0)
    slot = b % 2
    s_loc, tb = tile_ref.shape[1], x_ref.shape[0]

    def dmas(start, blk, s):
        _segment_dmas(start, n_experts, blk, (cntp_sm, loff_sm, dst_sm), tile_ref.at[s],
                      ys_ref, sem.at[s], False)

    def fetch(blk, s):
        tile_ref[s] = jnp.zeros(tile_ref.shape[1:], BF16)
        dmas(True, blk, s)

    @pl.when(b == 0)
    def _():
        fetch(0, 0)

    @pl.when(b + 1 < pl.num_programs(0))
    def _():
        fetch(b + 1, 1 - slot)

    info = info_ref[...]
    d1 = info[:, 0:1].astype(jnp.int32)
    d2 = info[:, 1:2].astype(jnp.int32)
    c = lax.broadcasted_iota(jnp.int32, (tb, s_loc), 1)
    pg = (jnp.where(c == d1, info[:, 2:3], 0.0) + jnp.where(c == d2, info[:, 3:4], 0.0)).astype(BF16)
    dmas(False, b, slot)
    y = _dot(pg, tile_ref[slot])
    xn = x_ref[...] + mod_ref[5:6, :] * y
    if final:
        ms = jnp.mean(xn * xn, axis=-1, keepdims=True)
        xn = xn * lax.rsqrt(ms + EPS) * fg_ref[...]
    o_ref[...] = xn


def _combine_call(cntp, loff, dst, xres, info, mod_l, ys, final_g, *, n_blocks, n_experts,
                  s_loc, mod_index):
    d = xres.shape[1]
    tb = ROUTE_BLOCK
    final = final_g is not None
    xspec = pl.BlockSpec((tb, d), lambda i, *_: (i, 0))
    in_specs = [
        xspec,
        pl.BlockSpec((tb, ROUTER_LANES), lambda i, *_: (i, 0)),
        pl.BlockSpec((None, 6, d), lambda i, *_: (mod_index(i), 0, 0)),
    ]
    args = [xres, info, mod_l]
    if final:
        in_specs.append(pl.BlockSpec((1, d), lambda i, *_: (0, 0)))
        args.append(final_g)
    in_specs.append(pl.BlockSpec(memory_space=pl.ANY))
    args.append(ys)
    out_rows = n_blocks * tb if final else xres.shape[0]
    return pl.pallas_call(
        functools.partial(_combine_kernel, n_experts, final),
        grid_spec=pltpu.PrefetchScalarGridSpec(
            num_scalar_prefetch=3,
            grid=(n_blocks,),
            in_specs=in_specs,
            out_specs=xspec,
            scratch_shapes=[pltpu.VMEM((2, s_loc, d), BF16), pltpu.SemaphoreType.DMA((2,))],
        ),
        out_shape=jax.ShapeDtypeStruct((out_rows, d), F32),
        input_output_aliases={} if final else {3: 0},
        compiler_params=_cparams(1, 48),
        name="moe_combine",
    )(cntp, loff, dst, *args)


def _moe_layout(cnt, n_tiles, tmx):
    cntp = (cnt + (SEG_ALIGN - 1)) // SEG_ALIGN * SEG_ALIGN
    loff = jnp.cumsum(cntp, axis=1) - cntp
    total = jnp.sum(cntp, axis=0)
    region = (total + (tmx - 1)) // tmx * tmx
    region_end = jnp.cumsum(region)
    dst = (region_end - region)[None, :] + jnp.cumsum(cntp, axis=0) - cntp
    tile_start = jnp.arange(n_tiles, dtype=jnp.int32) * tmx
    valid = tile_start < region_end[-1]
    last_tile = jnp.maximum(region_end[-1] // tmx - 1, 0) * tmx
    start_eff = jnp.where(valid, tile_start, last_tile)
    owner = jnp.sum((start_eff[:, None] >= region_end[None, :]).astype(jnp.int32), axis=1)
    flat = lambda v: v.reshape(-1).astype(jnp.int32)
    return flat(cntp), flat(loff), flat(dst), owner.astype(jnp.int32), valid.astype(jnp.int32)


def _moe_call(xres, mod_l, norm_g, rw, rb, wg, wu, wd, final_g, *, layer, n_blocks, mod_index):
    d = xres.shape[1]
    n_experts = wg.shape[1]
    tmx = EXPERT_ROWS
    block_rows = 2 * ROUTE_BLOCK + n_experts * (SEG_ALIGN - 1)
    s_loc = -(-block_rows // LANES) * LANES
    max_rows = n_blocks * block_rows + n_experts * (tmx - SEG_ALIGN)
    n_tiles = max_rows // tmx
    hb, info, infot, cnt = _route_call(xres, mod_l, norm_g, rw, rb, n_blocks=n_blocks,
                                       n_experts=n_experts, mod_index=mod_index)
    cnt = cnt[:, 0, :n_experts].astype(jnp.int32)
    cntp, loff, dst, owner, valid = _moe_layout(cnt, n_tiles, tmx)
    xs = _sort_call(cntp, loff, dst, hb, infot, jnp.zeros((n_tiles * tmx, d), BF16),
                    n_blocks=n_blocks, n_experts=n_experts, s_loc=s_loc)
    ys = _expert_call(owner, valid, xs, wg, wu, wd, layer=layer, tmx=tmx)
    return _combine_call(cntp, loff, dst, xres, info, mod_l, ys, final_g, n_blocks=n_blocks,
                         n_experts=n_experts, s_loc=s_loc, mod_index=mod_index)


def _final_kernel(x_ref, g_ref, o_ref):
    x = x_ref[...]
    ms = jnp.mean(x * x, axis=-1, keepdims=True)
    o_ref[...] = x * lax.rsqrt(ms + EPS) * g_ref[...]


def _final_call(xres, g, *, rows, tm):
    d = xres.shape[1]
    spec = pl.BlockSpec((tm, d), lambda i: (i, 0))
    return pl.pallas_call(
        _final_kernel,
        grid=(rows // tm,),
        in_specs=[spec, pl.BlockSpec((1, d), lambda i: (0, 0))],
        out_specs=spec,
        out_shape=jax.ShapeDtypeStruct((rows, d), F32),
        compiler_params=_cparams(1, 48),
        name="final_norm",
    )(xres, g)


def kernel(x, c, ctx, c_ctx, ada_w, ada_b, norm_mix_g, norm_ffn_g, w_in, w_out, conv_w,
           rec_conv_w, rec_conv_b, lru_wa, lru_ba, lru_wi, lru_bi, lru_lam,
           dense_wg, dense_wu, dense_wd, router_w, router_b, moe_wg, moe_wu, moe_wd, final_g):
    batch, n_lat, d = x.shape
    n_ctx = ctx.shape[1]
    depth = ada_w.shape[0]
    n_experts = router_w.shape[2]
    rows_lat = batch * n_lat
    rows_ctx = batch * n_ctx
    tm = min(MIX_ROWS, n_lat)
    tiles_per_b = n_lat // tm
    ctx_block0 = rows_lat // n_ctx
    tb = ROUTE_BLOCK
    assert n_lat % tb == 0 and rows_ctx % tb == 0 and batch < 8

    xres = jnp.concatenate([x.reshape(rows_lat, d), ctx.reshape(rows_ctx, d)], axis=0)
    cvec =jnp.zeros((8, d), F32).at[:batch].set(c).at[batch].set(c_ctx)
    mod = _ada_call(cvec, ada_w, ada_b).reshape(depth, 8, 6, d)

    def lat_index(i):
        return i // tiles_per_b

    def ctx_index(i):
        return batch

    def all_index(i):
        return jnp.where(i < rows_lat // tb, (i * tb) // n_lat, batch)

    rw = jnp.pad(router_w, ((0, 0), (0, 0), (0, ROUTER_LANES - n_experts)))
    rb = jnp.pad(router_b, ((0, 0), (0, ROUTER_LANES - n_experts)))[:, None, :]

    w_in_b = w_in.astype(BF16)
    w_out_b = w_out.astype(BF16)
    dense_b = [w.astype(BF16) for w in (dense_wg, dense_wu, dense_wd)]
    moe_b = [w.astype(BF16) for w in (moe_wg, moe_wu, moe_wd)]

    out = None
    for l in range(depth):
        last = l == depth - 1
        mod_l = mod[l]
        gmix = norm_mix_g[l][None, :]
        gffn = norm_ffn_g[l][None, :]

        cb_l, u_l, rx_l, g_l = _proj_call(xres, mod_l, gmix, w_in_b, layer=l, tm=tm,
                                          n_tiles=rows_lat // tm, first_block=0,
                                          mod_index=lat_index)
        cb_c, u_c, rx_c, g_c = _proj_call(xres, mod_l, gmix, w_in_b, layer=l, tm=n_ctx,
                                          n_tiles=batch, first_block=ctx_block0,
                                          mod_index=ctx_index)
        wgate, bgate, lam_b = _gate_params(lru_wa[l], lru_ba[l], lru_wi[l], lru_bi[l], lru_lam[l])
        rec_l, rec_c = _scan_call(rx_l, rx_c, g_l, g_c, rec_conv_w[l], rec_conv_b[l][None, :],
                                  wgate, bgate, lam_b, batch=batch)
        xres = _out_call(xres, cb_l, u_l, rec_l, w_out_b, conv_w[l], mod_l, layer=l, latent=True,
                         tm=tm, n_tiles=rows_lat // tm, first_block=0, tiles_per_b=tiles_per_b,
                         mod_index=lat_index)
        if not last:
            xres = _out_call(xres, cb_c, u_c, rec_c, w_out_b, conv_w[l], mod_l, layer=l,
                             latent=False, tm=n_ctx, n_tiles=batch, first_block=ctx_block0,
                             tiles_per_b=1, mod_index=ctx_index)

        j = l // 2
        if l % 2 == 0:
            xres = _ffn_call(xres, mod_l, gffn, *dense_b, layer=j, tm=tb, mod_index=all_index)
        else:
            n_blocks = rows_lat // tb if last else (rows_lat + rows_ctx) // tb
            res = _moe_call(xres, mod_l, gffn, rw[j], rb[j], *moe_b,
                            final_g[None, :] if last else None, layer=j, n_blocks=n_blocks,
                            mod_index=all_index)
            if last:
                out = res
            else:
                xres = res

    if out is None:
        out = _final_call(xres, final_g[None, :], rows=rows_lat, tm=tm)
    return out.reshape(batch, n_lat, d)
```
